```python
import math
import jax, jax.numpy as jnp
from jax import lax
import numpy as np

D_MODEL = 2048
BATCH = 2
SEQ = 8192
DEPTH = 4
DEC_BATCH = 4
DEC_SEQ = 8192
PAST_LEN = 128

N_MIXERS = 3
HEAD_DIM = 128
ROPE_THETA = 500000.0
ROT_FRAC = 4
NORM_EPS = 1e-6
Q_BLOCK = 128
NEG = -1e30

A_COMP = HEAD_DIM
A_VDIM = 2 * HEAD_DIM
A_HEADS = D_MODEL // A_VDIM
A_WIDTH = A_HEADS * A_VDIM

B_HEADS = D_MODEL // HEAD_DIM
B_KV_HEADS = B_HEADS // 4
B_WINDOW = 128

C_CONFIGS = ((128, 1), (512, 4), (2048, 16))
C_GROUPS = len(C_CONFIGS)
C_HEADS = D_MODEL // HEAD_DIM // 2
C_WIDTH = C_HEADS * HEAD_DIM

FFN_DIM = 2 * D_MODEL
MOE_DIM = D_MODEL // 2
N_EXPERTS = 8
TOP_K = 2

N_A = (DEPTH + 2) // 3
N_B = (DEPTH + 1) // 3
N_C = DEPTH // 3
N_DENSE = (DEPTH + 1) // 2
N_MOE = DEPTH // 2

kernel_name = 'hybrid_bidir_encoder_adaln'


def rmsnorm(x, g):
    xf = x.astype(jnp.float32)
    y = xf * lax.rsqrt(jnp.mean(xf * xf, axis=-1, keepdims=True) + NORM_EPS)
    return y.astype(x.dtype) * g


def rope(x, pos, rot):
    half = rot // 2
    inv = ROPE_THETA ** (-jnp.arange(half, dtype=jnp.float32) * (2.0 / rot))
    ang = pos[:, None] * inv[None, :]
    cos = jnp.cos(ang)[None, :, None, :]
    sin = jnp.sin(ang)[None, :, None, :]
    x1 = x[..., :half].astype(jnp.float32)
    x2 = x[..., half:rot].astype(jnp.float32)
    xr = jnp.concatenate([x1 * cos - x2 * sin, x2 * cos + x1 * sin], axis=-1).astype(x.dtype)
    return jnp.concatenate([xr, x[..., rot:]], axis=-1)


def lambda_init(layer):
    return 0.8 - 0.6 * math.exp(-0.3 * layer)


def banded_attention(q, k, v, half, sink=None):
    N, L, Hkv, G, Dh = q.shape
    blk = half
    nb = -(-L // blk)
    Lp = nb * blk
    q = jnp.pad(q, ((0, 0), (0, Lp - L), (0, 0), (0, 0), (0, 0)))
    kv_pad = ((0, 0), (blk, Lp - L + blk), (0, 0), (0, 0))
    kp = jnp.pad(k, kv_pad).reshape(N, nb + 2, blk, Hkv, Dh)
    vp = jnp.pad(v, kv_pad).reshape(N, nb + 2, blk, Hkv, Dh)
    kw = jnp.concatenate([kp[:, :-2], kp[:, 1:-1], kp[:, 2:]], axis=2)
    vw = jnp.concatenate([vp[:, :-2], vp[:, 1:-1], vp[:, 2:]], axis=2)
    qb = q.reshape(N, nb, blk, Hkv, G, Dh)
    s = jnp.einsum('nbqhgd,nbjhd->nbhgqj', qb, kw,
                   preferred_element_type=jnp.float32) * (Dh ** -0.5)
    nbi = jnp.arange(nb)[:, None, None]
    qpos = nbi * blk + jnp.arange(blk)[None, :, None]
    kpos = (nbi - 1) * blk + jnp.arange(3 * blk)[None, None, :]
    valid = (jnp.abs(kpos - qpos) <= half) & (kpos >= 0) & (kpos < L)
    s = jnp.where(valid[None, :, None, None], s, NEG)
    m = s.max(axis=-1)
    if sink is not None:
        m = jnp.maximum(m, sink[:, :, None])
    e = jnp.exp(s - m[..., None])
    den = e.sum(axis=-1)
    if sink is not None:
        den = den + jnp.exp(sink[:, :, None] - m)
    o = jnp.einsum('nbhgqj,nbjhd->nbqhgd', (e / den[..., None]).astype(v.dtype), vw)
    lse = m + jnp.log(den)
    o = o.reshape(N, Lp, Hkv, G, Dh)[:, :L]
    lse = lse.transpose(0, 1, 4, 2, 3).reshape(N, Lp, Hkv, G)[:, :L]
    return o, lse


def diff_attention(q, k, v, lam):
    B, S, H, _, Dc = q.shape
    nb = S // Q_BLOCK
    qb = q.reshape(B, nb, Q_BLOCK, H, 2, Dc).transpose(1, 0, 2, 3, 4, 5)

    def block(qi):
        s = jnp.einsum('bqhcd,bkhcd->bhcqk', qi, k,
                       preferred_element_type=jnp.float32) * (Dc ** -0.5)
        p = jax.nn.softmax(s, axis=-1)
        a = p[:, :, 0] - lam * p[:, :, 1]
        return jnp.einsum('bhqk,bkhd->bqhd', a.astype(v.dtype), v)

    o = lax.map(block, qb)
    return o.transpose(1, 0, 2, 3, 4).reshape(B, S, H, v.shape[-1])


def mixer_a(h, w_in, w_out, lam_p, subln, lam0, pos):
    B, S, _ = h.shape
    qkv = h @ w_in
    q = qkv[..., :A_WIDTH].reshape(B, S, A_HEADS * 2, A_COMP)
    k = qkv[..., A_WIDTH:2 * A_WIDTH].reshape(B, S, A_HEADS * 2, A_COMP)
    v = qkv[..., 2 * A_WIDTH:].reshape(B, S, A_HEADS, A_VDIM)
    rot = A_COMP // ROT_FRAC
    q = rope(q, pos, rot).reshape(B, S, A_HEADS, 2, A_COMP)
    k = rope(k, pos, rot).reshape(B, S, A_HEADS, 2, A_COMP)
    lp = lam_p.astype(jnp.float32)
    lam = jnp.exp(jnp.sum(lp[0] * lp[1])) - jnp.exp(jnp.sum(lp[2] * lp[3])) + lam0
    o = diff_attention(q, k, v, lam)
    o = rmsnorm(o, subln) * (1.0 - lam0)
    return o.reshape(B, S, A_WIDTH) @ w_out


def mixer_b(h, w_in, w_out, sink, pos):
    B, S, _ = h.shape
    qkv = h @ w_in
    nq = B_HEADS * HEAD_DIM
    nk = B_KV_HEADS * HEAD_DIM
    q = qkv[..., :nq].reshape(B, S, B_HEADS, HEAD_DIM)
    k = qkv[..., nq:nq + nk].reshape(B, S, B_KV_HEADS, HEAD_DIM)
    v = qkv[..., nq + nk:].reshape(B, S, B_KV_HEADS, HEAD_DIM)
    rot = HEAD_DIM // ROT_FRAC
    q = rope(q, pos, rot).reshape(B, S, B_KV_HEADS, B_HEADS // B_KV_HEADS, HEAD_DIM)
    k = rope(k, pos, rot)
    o, _ = banded_attention(q, k, v, B_WINDOW,
                            sink.astype(jnp.float32).reshape(B_KV_HEADS, B_HEADS // B_KV_HEADS))
    return o.reshape(B, S, nq) @ w_out


def dilated_attention(q, k, v, window, dil):
    B, S, H, Dh = q.shape
    L = S // dil

    def to_sub(t):
        return t.reshape(B, L, dil, H, Dh).swapaxes(1, 2).reshape(B * dil, L, H, Dh)

    o, lse = banded_attention(to_sub(q)[:, :, :, None, :], to_sub(k), to_sub(v),
                              window // (2 * dil))
    o = o[:, :, :, 0].reshape(B, dil, L, H, Dh).swapaxes(1, 2).reshape(B, S, H, Dh)
    lse = lse[..., 0].reshape(B, dil, L, H).swapaxes(1, 2).reshape(B, S, H)
    return o, lse


def mixer_c(h, w_in, w_out, pos):
    B, S, _ = h.shape
    qkv = (h @ w_in).reshape(B, S, 3, C_GROUPS, C_HEADS, HEAD_DIM)
    rot = HEAD_DIM // ROT_FRAC
    q = rope(qkv[:, :, 0].reshape(B, S, C_GROUPS * C_HEADS, HEAD_DIM), pos, rot)
    k = rope(qkv[:, :, 1].reshape(B, S, C_GROUPS * C_HEADS, HEAD_DIM), pos, rot)
    q = q.reshape(B, S, C_GROUPS, C_HEADS, HEAD_DIM)
    k = k.reshape(B, S, C_GROUPS, C_HEADS, HEAD_DIM)
    v = qkv[:, :, 2]
    outs, lses = [], []
    for g, (window, dil) in enumerate(C_CONFIGS):
        o_g, lse_g = dilated_attention(q[:, :, g], k[:, :, g], v[:, :, g], window, dil)
        outs.append(o_g)
        lses.append(lse_g)
    o = jnp.stack(outs)
    alpha = jax.nn.softmax(jnp.stack(lses), axis=0)
    o = jnp.sum(alpha[..., None].astype(o.dtype) * o, axis=0)
    return o.reshape(B, S, C_WIDTH) @ w_out


def swiglu(h, w_gu, w_down):
    f = w_down.shape[0]
    gu = h @ w_gu
    return (jax.nn.silu(gu[..., :f]) * gu[..., f:]) @ w_down


def moe_swiglu(h, router, w_gu, w_down):
    logits = jnp.einsum('bsd,de->bse', h, router, preferred_element_type=jnp.float32)
    top_v, top_i = lax.top_k(logits, TOP_K)
    top_w = jax.nn.softmax(top_v, axis=-1)
    gates = jnp.sum(jax.nn.one_hot(top_i, N_EXPERTS, dtype=jnp.float32) * top_w[..., None],
                    axis=-2)
    y = jnp.zeros_like(h)
    for e in range(N_EXPERTS):
        y = y + gates[..., e:e + 1].astype(h.dtype) * swiglu(h, w_gu[e], w_down[e])
    return y


def trunk(x, c, ada_w, ada_b, norm_mix, norm_ffn, a_w_in, a_w_out, a_lambda, a_subln,
          b_w_in, b_w_out, b_sink, c_w_in, c_w_out, f_w_gu, f_w_down,
          moe_router, moe_w_gu, moe_w_down, final_norm):
    B, S, _ = x.shape
    pos = jnp.arange(S, dtype=jnp.float32)
    cs = jax.nn.silu(c)
    for i in range(DEPTH):
        mod = cs @ ada_w[i] + ada_b[i]
        sh_m, sc_m, g_m, sh_f, sc_f, g_f = [t[:, None, :] for t in jnp.split(mod, 6, axis=-1)]
        h = rmsnorm(x, norm_mix[i]) * (1 + sc_m) + sh_m
        kind = i % N_MIXERS
        j = i // N_MIXERS
        if kind == 0:
            mix = mixer_a(h, a_w_in[j], a_w_out[j], a_lambda[j], a_subln[j], lambda_init(i), pos)
        elif kind == 1:
            mix = mixer_b(h, b_w_in[j], b_w_out[j], b_sink[j], pos)
        else:
            mix = mixer_c(h, c_w_in[j], c_w_out[j], pos)
        x = x + g_m * mix
        h = rmsnorm(x, norm_ffn[i]) * (1 + sc_f) + sh_f
        if i % 2 == 0:
            f = swiglu(h, f_w_gu[i // 2], f_w_down[i // 2])
        else:
            f = moe_swiglu(h, moe_router[i // 2], moe_w_gu[i // 2], moe_w_down[i // 2])
        x = x + g_f * f
    return rmsnorm(x, final_norm)


def setup_inputs(seed: int = 0) -> dict:
    key = jax.random.key(seed)
    ks = jax.random.split(key, 24)

    def nrm(k, shape, scale=1.0):
        return jax.random.normal(k, shape, jnp.float32) * scale

    D = D_MODEL
    sd = D ** -0.5
    return {
        'x_prompt': nrm(ks[0], (BATCH, SEQ, D)),
        'x_sample': nrm(ks[1], (DEC_BATCH, DEC_SEQ, D)),
        'c_prompt': nrm(ks[2], (BATCH, D)),
        'c_sample': nrm(ks[3], (DEC_BATCH, D)),
        'ada_w': nrm(ks[4], (DEPTH, D, 6 * D), 0.5 * sd),
        'ada_b': nrm(ks[5], (DEPTH, 6 * D), 0.02),
        'norm_mix': 1.0 + nrm(ks[6], (DEPTH, D), 0.02),
        'norm_ffn': 1.0 + nrm(ks[7], (DEPTH, D), 0.02),
        'a_w_in': nrm(ks[8], (N_A, D, 3 * A_WIDTH), sd),
        'a_w_out': nrm(ks[9], (N_A, A_WIDTH, D), A_WIDTH ** -0.5),
        'a_lambda': nrm(ks[10], (N_A, 4, A_COMP), 0.1),
        'a_subln': 1.0 + nrm(ks[11], (N_A, A_VDIM), 0.02),
        'b_w_in': nrm(ks[12], (N_B, D, (B_HEADS + 2 * B_KV_HEADS) * HEAD_DIM), sd),
        'b_w_out': nrm(ks[13], (N_B, B_HEADS * HEAD_DIM, D), (B_HEADS * HEAD_DIM) ** -0.5),
        'b_sink': nrm(ks[14], (N_B, B_HEADS), 0.5),
        'c_w_in': nrm(ks[15], (N_C, D, 3 * C_GROUPS * C_WIDTH), sd),
        'c_w_out': nrm(ks[16], (N_C, C_WIDTH, D), C_WIDTH ** -0.5),
        'f_w_gu': nrm(ks[17], (N_DENSE, D, 2 * FFN_DIM), sd),
        'f_w_down': nrm(ks[18], (N_DENSE, FFN_DIM, D), FFN_DIM ** -0.5),
        'moe_router': nrm(ks[19], (N_MOE, D, N_EXPERTS), sd),
        'moe_w_gu': nrm(ks[20], (N_MOE, N_EXPERTS, D, 2 * MOE_DIM), sd),
        'moe_w_down': nrm(ks[21], (N_MOE, N_EXPERTS, MOE_DIM, D), MOE_DIM ** -0.5),
        'final_norm': 1.0 + nrm(ks[22], (D,), 0.02),
    }


def reference(x_prompt, x_sample, c_prompt, c_sample, ada_w, ada_b, norm_mix, norm_ffn,
              a_w_in, a_w_out, a_lambda, a_subln, b_w_in, b_w_out, b_sink, c_w_in, c_w_out,
              f_w_gu, f_w_down, moe_router, moe_w_gu, moe_w_down, final_norm):
    y_prompt = trunk(x_prompt, c_prompt, ada_w, ada_b, norm_mix, norm_ffn, a_w_in, a_w_out,
                     a_lambda, a_subln, b_w_in, b_w_out, b_sink, c_w_in, c_w_out,
                     f_w_gu, f_w_down, moe_router, moe_w_gu, moe_w_down, final_norm)
    y_sample = trunk(x_sample, c_sample, ada_w, ada_b, norm_mix, norm_ffn, a_w_in, a_w_out,
                     a_lambda, a_subln, b_w_in, b_w_out, b_sink, c_w_in, c_w_out,
                     f_w_gu, f_w_down, moe_router, moe_w_gu, moe_w_down, final_norm)
    return (y_prompt, y_sample)
```

```python
import functools
import math

import jax
import jax.numpy as jnp
from jax import lax
from jax.experimental import pallas as pl
from jax.experimental.pallas import tpu as pltpu

F32 = jnp.float32
BF16 = jnp.bfloat16

D_MODEL = 2048
DEPTH = 4
HEAD_DIM = 128
ROPE_THETA = 500000.0
ROT = HEAD_DIM // 4
NORM_EPS = 1e-6
NEG = -1e30
A_HEADS = 8
B_HEADS = 16
B_KV_HEADS = 4
B_WINDOW = 128
C_CONFIGS = ((128, 1), (512, 4), (2048, 16))
C_HEADS = 8
C_WIDTH = C_HEADS * HEAD_DIM
FFN_DIM = 2 * D_MODEL
MOE_DIM = D_MODEL // 2
N_EXPERTS = 8

LOG2E = math.log2(math.e)
Q_SCALE = HEAD_DIM ** -0.5 * LOG2E

LANES = 128
VMEM_LIMIT = 56 * 1024 * 1024

NT_DIMS = (((1,), (1,)), ((), ()))


def _params(sem, vmem=VMEM_LIMIT):
    return pltpu.CompilerParams(dimension_semantics=sem, vmem_limit_bytes=vmem)


def _dot(a, b):
    return jnp.dot(a, b, preferred_element_type=F32)


def _split_bf16(x):
    hi = x.astype(BF16)
    lo = (x - hi.astype(F32)).astype(BF16)
    return hi, lo


def _dot3(a, b):
    a_hi, a_lo = _split_bf16(a)
    b_hi, b_lo = _split_bf16(b)
    return _dot(a_hi, b_hi) + (_dot(a_hi, b_lo) + _dot(a_lo, b_hi))


def _norm_mod(x, g, sc, sh):
    ms = jnp.mean(x * x, axis=-1, keepdims=True)
    y = x * lax.rsqrt(ms + NORM_EPS)
    return (y * g) * (1.0 + sc) + sh


def _adaln_kernel(c_ref, w_ref, b_ref, o_ref):
    c = c_ref[...]
    cs = c * jax.nn.sigmoid(c)
    o_ref[...] = _dot3(cs, w_ref[...]) + b_ref[...]


def adaln(c8, ada_w, ada_b, tn=512):
    depth, d, n = ada_w.shape
    return pl.pallas_call(
        _adaln_kernel,
        out_shape=jax.ShapeDtypeStruct((depth, 8, n), F32),
        grid=(depth, n // tn),
        in_specs=[
            pl.BlockSpec((8, d), lambda l, j: (0, 0)),
            pl.BlockSpec((None, d, tn), lambda l, j: (l, 0, j)),
            pl.BlockSpec((None, 1, tn), lambda l, j: (l, 0, j)),
        ],
        out_specs=pl.BlockSpec((None, 8, tn), lambda l, j: (l, 0, j)),
        compiler_params=_params(("arbitrary", "arbitrary")),
        name="adaln",
    )(c8, ada_w, ada_b.reshape(depth, 1, n))


def _rope_tables(seq):
    pos = jnp.arange(seq, dtype=F32)
    half = ROT // 2
    inv = ROPE_THETA ** (-jnp.arange(half, dtype=F32) * (2.0 / ROT))
    ang = pos[:, None] * inv[None, :]
    cos, sin = jnp.cos(ang), jnp.sin(ang)
    zeros = lambda n: jnp.zeros((seq, n), F32)
    cos_t = jnp.concatenate([cos, cos, jnp.ones((seq, HEAD_DIM - ROT), F32)], axis=1)
    sin_a = jnp.concatenate([zeros(half), sin, zeros(HEAD_DIM - ROT)], axis=1)
    sin_b = jnp.concatenate([-sin, zeros(HEAD_DIM - half)], axis=1)
    return cos_t, sin_a, sin_b


def _inproj_kernel(x_ref, g_ref, sc_ref, sh_ref, w_ref, cos_ref, sa_ref, sb_ref, o_ref, h_scr,
                   *, n_q, n_rope, tn):
    j = pl.program_id(1)

    @pl.when(j == 0)
    def _():
        h_scr[...] = _norm_mod(x_ref[...], g_ref[...], sc_ref[...], sh_ref[...]).astype(BF16)

    acc = _dot(h_scr[...], w_ref[...])

    @pl.when(j < n_rope)
    def _():
        scale = jnp.where(j < n_q, Q_SCALE, 1.0).astype(F32)
        cos = cos_ref[...] * scale
        sa = sa_ref[...] * scale
        sb = sb_ref[...] * scale
        for hh in range(tn // HEAD_DIM):
            a = acc[:, hh * HEAD_DIM:(hh + 1) * HEAD_DIM]
            r = a * cos + pltpu.roll(a, ROT // 2, 1) * sa + pltpu.roll(a, HEAD_DIM - ROT // 2, 1) * sb
            o_ref[:, hh * HEAD_DIM:(hh + 1) * HEAD_DIM] = r.astype(BF16)

    @pl.when(j >= n_rope)
    def _():
        o_ref[...] = acc.astype(BF16)


def inproj(x, mod3, g, w, tables, seq, *, q_cols, rope_cols, tm, tn):
    t, d = x.shape
    n = w.shape[1]
    tps = seq // tm
    kern = functools.partial(_inproj_kernel, n_q=q_cols // tn, n_rope=rope_cols // tn, tn=tn)
    mod_spec = lambda k: pl.BlockSpec((None, 1, d), lambda i, j: (i // tps, 0, k))
    tab_spec = pl.BlockSpec((tm, HEAD_DIM), lambda i, j: (i % tps, 0))
    return pl.pallas_call(
        kern,
        out_shape=jax.ShapeDtypeStruct((t, n), BF16),
        grid=(t // tm, n // tn),
        in_specs=[
            pl.BlockSpec((tm, d), lambda i, j: (i, 0)),
            pl.BlockSpec((1, d), lambda i, j: (0, 0)),
            mod_spec(1), mod_spec(0),
            pl.BlockSpec((d, tn), lambda i, j: (0, j)),
            tab_spec, tab_spec, tab_spec,
        ],
        out_specs=pl.BlockSpec((tm, tn), lambda i, j: (i, j)),
        scratch_shapes=[pltpu.VMEM((tm, d), BF16)],
        compiler_params=_params(("parallel", "arbitrary")),
        name="inproj",
    )(x, g.reshape(1, d), mod3, mod3, w, *tables)


def _attn_a_kernel(q_ref, k_ref, v_ref, lam_ref, subln_ref, o_ref, *, tk, lam0):
    seq = k_ref.shape[0]
    tq = q_ref.shape[0]
    dv = v_ref.shape[1]
    q = q_ref[...]
    q0 = q[:, :HEAD_DIM]
    q1 = q[:, HEAD_DIM:]

    def update(qc, kc, v, m, l, acc):
        s = lax.dot_general(qc, kc, NT_DIMS, preferred_element_type=F32)
        m_new = jnp.maximum(m, jnp.max(s, axis=1, keepdims=True))
        alpha = jnp.exp2(m - m_new)
        p = jnp.exp2(s - m_new)
        l = alpha * l + jnp.sum(p, axis=1, keepdims=True)
        acc = alpha * acc + _dot(p.astype(BF16), v)
        return m_new, l, acc

    def body(c, carry):
        m0, l0, a0, m1, l1, a1 = carry
        off = pl.multiple_of(c * tk, tk)
        k = k_ref[pl.ds(off, tk), :]
        v = v_ref[pl.ds(off, tk), :]
        m0, l0, a0 = update(q0, k[:, :HEAD_DIM], v, m0, l0, a0)
        m1, l1, a1 = update(q1, k[:, HEAD_DIM:], v, m1, l1, a1)
        return m0, l0, a0, m1, l1, a1

    m_init = jnp.full((tq, 1), NEG, F32)
    l_init = jnp.zeros((tq, 1), F32)
    a_init = jnp.zeros((tq, dv), F32)
    _, l0, a0, _, l1, a1 = lax.fori_loop(
        0, seq // tk, body, (m_init, l_init, a_init, m_init, l_init, a_init))

    lp = lam_ref[...].astype(F32)
    lam = (jnp.exp(jnp.sum(lp[0:1] * lp[1:2], axis=1, keepdims=True))
           - jnp.exp(jnp.sum(lp[2:3] * lp[3:4], axis=1, keepdims=True)) + lam0)
    o = a0 / l0 - lam * (a1 / l1)
    ms = jnp.mean(o * o, axis=1, keepdims=True)
    y = (o * lax.rsqrt(ms + NORM_EPS)) * subln_ref[...]
    o_ref[...] = (y * (1.0 - lam0)).astype(BF16)


def attn_a(qkv, lam_p, subln, nb, seq, lam0, *, tq, tk):
    t = qkv.shape[0]
    dv = 2 * HEAD_DIM
    nq = seq // tq
    kern = functools.partial(_attn_a_kernel, tk=tk, lam0=lam0)
    return pl.pallas_call(
        kern,
        out_shape=jax.ShapeDtypeStruct((t, A_HEADS * dv), BF16),
        grid=(nb, A_HEADS, nq),
        in_specs=[
            pl.BlockSpec((tq, dv), lambda b, h, i: (b * nq + i, h)),
            pl.BlockSpec((seq, dv), lambda b, h, i: (b, A_HEADS + h)),
            pl.BlockSpec((seq, dv), lambda b, h, i: (b, 2 * A_HEADS + h)),
            pl.BlockSpec((4, HEAD_DIM), lambda b, h, i: (0, 0)),
            pl.BlockSpec((1, dv), lambda b, h, i: (0, 0)),
        ],
        out_specs=pl.BlockSpec((tq, dv), lambda b, h, i: (b * nq + i, h)),
        compiler_params=_params(("parallel", "parallel", "arbitrary")),
        name="attn_a",
    )(qkv, qkv, qkv, lam_p, subln.reshape(1, dv))


def _band_kernel(*refs, half, tl, seq_len, n_kv, group, has_sink, emit_lse):
    if has_sink:
        sink_ref, refs = refs[0], refs[1:]
    q_ref, kp_ref, kc_ref, kn_ref, vp_ref, vc_ref, vn_ref = refs[:7]
    o_ref = refs[7]
    lse_ref = refs[8] if emit_lse else None
    i = pl.program_id(2)
    win = tl + 2 * half
    qpos = i * tl + lax.broadcasted_iota(jnp.int32, (tl, win), 0)
    kpos = i * tl - half + lax.broadcasted_iota(jnp.int32, (tl, win), 1)
    dist = jnp.abs(kpos - qpos)
    dist = jnp.where(kpos < 0, win, dist)
    dist = jnp.where(kpos >= seq_len, win, dist)
    valid = dist <= half
    for kv in range(n_kv):
        cs = slice(kv * HEAD_DIM, (kv + 1) * HEAD_DIM)
        kw = jnp.concatenate([kp_ref[:, cs], kc_ref[:, cs], kn_ref[:, cs]], axis=0)
        vw = jnp.concatenate([vp_ref[:, cs], vc_ref[:, cs], vn_ref[:, cs]], axis=0)
        for g in range(group):
            hd = kv * group + g
            hs = slice(hd * HEAD_DIM, (hd + 1) * HEAD_DIM)
            s = lax.dot_general(q_ref[:, hs], kw, NT_DIMS, preferred_element_type=F32)
            s = jnp.where(valid, s, NEG)
            m = jnp.max(s, axis=1, keepdims=True)
            if has_sink:
                sk = sink_ref[hd] * LOG2E
                m = jnp.maximum(m, sk)
            e = jnp.exp2(s - m)
            den = jnp.sum(e, axis=1, keepdims=True)
            if has_sink:
                den = den + jnp.exp2(sk - m)
            o = _dot(e.astype(BF16), vw) / den
            o_ref[:, hs] = o.astype(o_ref.dtype)
            if emit_lse:
                lse_ref[:, hs] = jnp.broadcast_to(m + jnp.log2(den), (tl, HEAD_DIM))


def band_attn(q_arr, k_arr, v_arr, *, q_col, k_col, v_col, n_kv, group, half, tl, dil, sink=None,
              emit_lse=False):
    nb, seq_len, _ = q_arr.shape
    wq = n_kv * group * HEAD_DIM
    wk = n_kv * HEAD_DIM
    nq = seq_len // tl
    hb = tl // half
    nh = seq_len // half
    q_blocks = q_arr.shape[2] // dil // wq
    k_blocks = k_arr.shape[2] // dil // wk
    kern = functools.partial(_band_kernel, half=half, tl=tl, seq_len=seq_len, n_kv=n_kv, group=group,
                             has_sink=sink is not None, emit_lse=emit_lse)

    def halo(col, blocks):
        prev = pl.BlockSpec((None, half, wk),
                            lambda b, r, i: (b, jnp.maximum(i * hb - 1, 0), r * blocks + col))
        cur = pl.BlockSpec((None, tl, wk), lambda b, r, i: (b, i, r * blocks + col))
        nxt = pl.BlockSpec((None, half, wk),
                           lambda b, r, i: (b, jnp.minimum((i + 1) * hb, nh - 1), r * blocks + col))
        return [prev, cur, nxt]

    in_specs = [pl.BlockSpec((None, tl, wq), lambda b, r, i: (b, i, r * q_blocks + q_col))]
    in_specs += halo(k_col, k_blocks) + halo(v_col, k_blocks)
    args = [q_arr, k_arr, k_arr, k_arr, v_arr, v_arr, v_arr]
    if sink is not None:
        in_specs = [pl.BlockSpec(memory_space=pltpu.SMEM)] + in_specs
        args = [sink] + args
    out_spec = pl.BlockSpec((None, tl, wq), lambda b, r, i: (b, i, r))
    o_shape = jax.ShapeDtypeStruct((nb, seq_len, dil * wq), BF16)
    if emit_lse:
        out_shape = (o_shape, jax.ShapeDtypeStruct((nb, seq_len, dil * wq), F32))
        out_specs = (out_spec, out_spec)
    else:
        out_shape, out_specs = o_shape, out_spec
    return pl.pallas_call(
        kern,
        out_shape=out_shape,
        grid=(nb, dil, nq),
        in_specs=in_specs,
        out_specs=out_specs,
        compiler_params=_params(("parallel", "parallel", "arbitrary")),
        name="band_attn",
    )(*args)


def _outproj_kernel(a_ref, w_ref, x_ref, gate_ref, o_ref):
    o_ref[...] = x_ref[...] + gate_ref[...] * _dot(a_ref[...], w_ref[...])


def outproj(a, w, x, mod3, gate_chunk, seq, *, tm, tn):
    t, k = a.shape
    d = w.shape[1]
    tps = seq // tm
    nj = d // tn
    return pl.pallas_call(
        _outproj_kernel,
        out_shape=jax.ShapeDtypeStruct((t, d), F32),
        grid=(t // tm, nj),
        in_specs=[
            pl.BlockSpec((tm, k), lambda i, j: (i, 0)),
            pl.BlockSpec((k, tn), lambda i, j: (0, j)),
            pl.BlockSpec((tm, tn), lambda i, j: (i, j)),
            pl.BlockSpec((None, 1, tn), lambda i, j: (i // tps, 0, gate_chunk * nj + j)),
        ],
        out_specs=pl.BlockSpec((tm, tn), lambda i, j: (i, j)),
        compiler_params=_params(("parallel", "arbitrary")),
        name="outproj",
    )(a, w, x, mod3)


def _merge_outproj_kernel(o0_ref, o1_ref, o2_ref, l0_ref, l1_ref, l2_ref, w_ref, x_ref, gate_ref,
                          o_ref, a_scr):
    j = pl.program_id(1)

    @pl.when(j == 0)
    def _():
        l0, l1, l2 = l0_ref[...], l1_ref[...], l2_ref[...]
        m = jnp.maximum(jnp.maximum(l0, l1), l2)
        e0, e1, e2 = jnp.exp2(l0 - m), jnp.exp2(l1 - m), jnp.exp2(l2 - m)
        den = e0 + e1 + e2
        merged = ((e0 / den) * o0_ref[...].astype(F32) + (e1 / den) * o1_ref[...].astype(F32)
                  + (e2 / den) * o2_ref[...].astype(F32))
        a_scr[...] = merged.astype(BF16)

    o_ref[...] = x_ref[...] + gate_ref[...] * _dot(a_scr[...], w_ref[...])


def merge_outproj(os_, lses, w, x, mod3, gate_chunk, seq, *, tm, tn):
    t, k = os_[0].shape
    d = w.shape[1]
    tps = seq // tm
    nj = d // tn
    row = pl.BlockSpec((tm, k), lambda i, j: (i, 0))
    return pl.pallas_call(
        _merge_outproj_kernel,
        out_shape=jax.ShapeDtypeStruct((t, d), F32),
        grid=(t // tm, nj),
        in_specs=[row] * 6 + [
            pl.BlockSpec((k, tn), lambda i, j: (0, j)),
            pl.BlockSpec((tm, tn), lambda i, j: (i, j)),
            pl.BlockSpec((None, 1, tn), lambda i, j: (i // tps, 0, gate_chunk * nj + j)),
        ],
        out_specs=pl.BlockSpec((tm, tn), lambda i, j: (i, j)),
        scratch_shapes=[pltpu.VMEM((tm, k), BF16)],
        compiler_params=_params(("parallel", "arbitrary")),
        name="merge_outproj",
    )(*os_, *lses, w, x, mod3)


def _silu_mul(g, u):
    return (g * jax.nn.sigmoid(g)) * u


def _ffn_up_kernel(x_ref, g_ref, sc_ref, sh_ref, wg_ref, wu_ref, o_ref, h_scr):
    j = pl.program_id(1)

    @pl.when(j == 0)
    def _():
        h_scr[...] = _norm_mod(x_ref[...], g_ref[...], sc_ref[...], sh_ref[...]).astype(BF16)

    h = h_scr[...]
    o_ref[...] = _silu_mul(_dot(h, wg_ref[...]), _dot(h, wu_ref[...])).astype(BF16)


def ffn_up(x, mod3, g, w_gu, seq, *, tm, tn):
    t, d = x.shape
    f = w_gu.shape[1] // 2
    tps = seq // tm
    nj = f // tn
    mod_spec = lambda k: pl.BlockSpec((None, 1, d), lambda i, j: (i // tps, 0, k))
    return pl.pallas_call(
        _ffn_up_kernel,
        out_shape=jax.ShapeDtypeStruct((t, f), BF16),
        grid=(t // tm, nj),
        in_specs=[
            pl.BlockSpec((tm, d), lambda i, j: (i, 0)),
            pl.BlockSpec((1, d), lambda i, j: (0, 0)),
            mod_spec(4), mod_spec(3),
            pl.BlockSpec((d, tn), lambda i, j: (0, j)),
            pl.BlockSpec((d, tn), lambda i, j: (0, nj + j)),
        ],
        out_specs=pl.BlockSpec((tm, tn), lambda i, j: (i, j)),
        scratch_shapes=[pltpu.VMEM((tm, d), BF16)],
        compiler_params=_params(("parallel", "arbitrary")),
        name="ffn_up",
    )(x, g.reshape(1, d), mod3, mod3, w_gu, w_gu)


def _top2_gates(logits):
    lane = lax.broadcasted_iota(jnp.int32, logits.shape, 1).astype(F32)
    logits = jnp.where(lane < N_EXPERTS, logits, NEG)
    v1 = jnp.max(logits, axis=1, keepdims=True)
    i1 = jnp.min(jnp.where(logits == v1, lane, float(LANES)), axis=1, keepdims=True)
    rest = jnp.where(lane == i1, NEG, logits)
    v2 = jnp.max(rest, axis=1, keepdims=True)
    i2 = jnp.min(jnp.where(rest == v2, lane, float(LANES)), axis=1, keepdims=True)
    e2 = jnp.exp(v2 - v1)
    w1 = 1.0 / (1.0 + e2)
    w2 = e2 / (1.0 + e2)
    return jnp.where(lane == i1, w1, 0.0) + jnp.where(lane == i2, w2, 0.0)


def _moe_up_kernel(x_ref, g_ref, sc_ref, sh_ref, r_ref, wg_ref, wu_ref, o_ref, h_scr, gate_scr,
                   *, tiles_per_expert):
    j = pl.program_id(1)

    @pl.when(j == 0)
    def _():
        h = _norm_mod(x_ref[...], g_ref[...], sc_ref[...], sh_ref[...])
        h_scr[...] = h.astype(BF16)
        gates = _top2_gates(_dot3(h, r_ref[...]))
        lane = lax.broadcasted_iota(jnp.int32, gates.shape, 1)
        for e in range(N_EXPERTS):
            col = jnp.sum(jnp.where(lane == e, gates, 0.0), axis=1, keepdims=True)
            gate_scr[e] = jnp.broadcast_to(col, gates.shape)

    h = h_scr[...]
    act = _silu_mul(_dot(h, wg_ref[...]), _dot(h, wu_ref[...]))
    gate = gate_scr[j // tiles_per_expert]
    tn = act.shape[1]
    for c in range(tn // LANES):
        o_ref[:, c * LANES:(c + 1) * LANES] = (act[:, c * LANES:(c + 1) * LANES] * gate).astype(BF16)


def moe_up(x, mod3, g, router, w_gu, seq, *, tm, tn):
    t, d = x.shape
    n_e, _, two_f = w_gu.shape
    f = two_f // 2
    tps = seq // tm
    tpe = f // tn
    mod_spec = lambda k: pl.BlockSpec((None, 1, d), lambda i, j: (i // tps, 0, k))
    router_p = jnp.pad(router, ((0, 0), (0, LANES - n_e)))
    kern = functools.partial(_moe_up_kernel, tiles_per_expert=tpe)
    return pl.pallas_call(
        kern,
        out_shape=jax.ShapeDtypeStruct((t, n_e * f), BF16),
        grid=(t // tm, n_e * tpe),
        in_specs=[
            pl.BlockSpec((tm, d), lambda i, j: (i, 0)),
            pl.BlockSpec((1, d), lambda i, j: (0, 0)),
            mod_spec(4), mod_spec(3),
            pl.BlockSpec((d, LANES), lambda i, j: (0, 0)),
            pl.BlockSpec((None, d, tn), lambda i, j: (j // tpe, 0, j % tpe)),
            pl.BlockSpec((None, d, tn), lambda i, j: (j // tpe, 0, tpe + j % tpe)),
        ],
        out_specs=pl.BlockSpec((tm, tn), lambda i, j: (i, j)),
        scratch_shapes=[pltpu.VMEM((tm, d), BF16), pltpu.VMEM((N_EXPERTS, tm, LANES), F32)],
        compiler_params=_params(("parallel", "arbitrary")),
        name="moe_up",
    )(x, g.reshape(1, d), mod3, mod3, router_p, w_gu, w_gu)


def _final_norm_kernel(x_ref, g_ref, o_ref):
    x = x_ref[...]
    ms = jnp.mean(x * x, axis=-1, keepdims=True)
    o_ref[...] = (x * lax.rsqrt(ms + NORM_EPS)) * g_ref[...]


def final_norm(x, g, *, tm):
    t, d = x.shape
    return pl.pallas_call(
        _final_norm_kernel,
        out_shape=jax.ShapeDtypeStruct((t, d), F32),
        grid=(t // tm,),
        in_specs=[pl.BlockSpec((tm, d), lambda i: (i, 0)), pl.BlockSpec((1, d), lambda i: (0, 0))],
        out_specs=pl.BlockSpec((tm, d), lambda i: (i, 0)),
        compiler_params=_params(("parallel",)),
        name="final_norm",
    )(x, g.reshape(1, d))


def _lambda_init(layer):
    return 0.8 - 0.6 * math.exp(-0.3 * layer)


def _mixer_a(x, mod3, g, w_in, w_out, lam_p, subln, tables, nb, seq, layer, cfg):
    qkv = inproj(x, mod3, g, w_in, tables, seq, q_cols=D_MODEL, rope_cols=2 * D_MODEL,
                 tm=cfg["tm"], tn=cfg["tn_in"])
    o = attn_a(qkv, lam_p, subln, nb, seq, _lambda_init(layer), tq=cfg["tq_a"], tk=cfg["tk_a"])
    return outproj(o, w_out, x, mod3, 2, seq, tm=cfg["tm"], tn=cfg["tn_out"])


def _mixer_b(x, mod3, g, w_in, w_out, sink, tables, nb, seq, cfg):
    nq = B_HEADS * HEAD_DIM
    qkv = inproj(x, mod3, g, w_in, tables, seq, q_cols=nq, rope_cols=nq + B_KV_HEADS * HEAD_DIM,
                 tm=cfg["tm"], tn=cfg["tn_in_b"])
    qkv3 = qkv.reshape(nb, seq, qkv.shape[1])
    o = band_attn(qkv3, qkv3, qkv3, q_col=0, k_col=4, v_col=5, n_kv=B_KV_HEADS,
                  group=B_HEADS // B_KV_HEADS, half=B_WINDOW, tl=cfg["tl_b"], dil=1, sink=sink)
    return outproj(o.reshape(nb * seq, nq), w_out, x, mod3, 2, seq, tm=cfg["tm"], tn=cfg["tn_out"])


def _mixer_c(x, mod3, g, w_in, w_out, tables, nb, seq, cfg):
    n_groups = len(C_CONFIGS)
    width = 3 * n_groups * C_WIDTH
    qkv = inproj(x, mod3, g, w_in, tables, seq, q_cols=n_groups * C_WIDTH,
                 rope_cols=2 * n_groups * C_WIDTH, tm=cfg["tm"], tn=cfg["tn_in"])
    os_, lses = [], []
    for gi, (window, dil) in enumerate(C_CONFIGS):
        view = qkv.reshape(nb, seq // dil, dil * width)
        o, lse = band_attn(view, view, view, q_col=gi, k_col=n_groups + gi, v_col=2 * n_groups + gi,
                           n_kv=C_HEADS, group=1, half=window // (2 * dil), tl=cfg["tl_c"], dil=dil,
                           emit_lse=True)
        os_.append(o.reshape(nb * seq, C_WIDTH))
        lses.append(lse.reshape(nb * seq, C_WIDTH))
    return merge_outproj(os_, lses, w_out, x, mod3, 2, seq, tm=cfg["tm_merge"], tn=cfg["tn_out"])


def _default_cfg(seq):
    return dict(tm=min(512, seq), tn_in=1024, tn_in_b=512, tn_out=512, tq_a=min(256, seq),
                tk_a=min(512, seq), tl_b=min(256, seq), tl_c=min(256, seq // C_CONFIGS[-1][1]),
                tm_merge=min(512, seq), tn_up=512, tm_down=min(512, seq))


def trunk_all(x, c8, nb, seq, ada_w, ada_b, norm_mix, norm_ffn, a_w_in, a_w_out, a_lambda, a_subln,
              b_w_in, b_w_out, b_sink, c_w_in, c_w_out, f_w_gu, f_w_down, moe_router, moe_w_gu,
              moe_w_down, final_norm_g, cfg=None):
    cfg = cfg or _default_cfg(seq)
    bf = lambda w: w.astype(BF16)
    mod_all = adaln(c8, ada_w, ada_b)
    tables = _rope_tables(seq)
    for i in range(DEPTH):
        mod3 = mod_all[i].reshape(8, 1, 6 * D_MODEL)
        kind, jm = i % 3, i // 3
        if kind == 0:
            x = _mixer_a(x, mod3, norm_mix[i], bf(a_w_in[jm]), bf(a_w_out[jm]), a_lambda[jm],
                         a_subln[jm], tables, nb, seq, i, cfg)
        elif kind == 1:
            x = _mixer_b(x, mod3, norm_mix[i], bf(b_w_in[jm]), bf(b_w_out[jm]), b_sink[jm], tables,
                         nb, seq, cfg)
        else:
            x = _mixer_c(x, mod3, norm_mix[i], bf(c_w_in[jm]), bf(c_w_out[jm]), tables, nb, seq, cfg)
        jf = i // 2
        if i % 2 == 0:
            act = ffn_up(x, mod3, norm_ffn[i], bf(f_w_gu[jf]), seq, tm=cfg["tm"], tn=cfg["tn_up"])
            x = outproj(act, bf(f_w_down[jf]), x, mod3, 5, seq, tm=cfg["tm"], tn=cfg["tn_out"])
        else:
            act = moe_up(x, mod3, norm_ffn[i], moe_router[jf], bf(moe_w_gu[jf]), seq,
                         tm=cfg["tm"], tn=cfg["tn_up"])
            w_down = bf(moe_w_down[jf]).reshape(N_EXPERTS * MOE_DIM, D_MODEL)
            x = outproj(act, w_down, x, mod3, 5, seq, tm=cfg["tm_down"], tn=cfg["tn_out"])
    return final_norm(x, final_norm_g, tm=cfg["tm_down"])


def kernel(x_prompt, x_sample, c_prompt, c_sample, ada_w, ada_b, norm_mix, norm_ffn, a_w_in, a_w_out,
           a_lambda, a_subln, b_w_in, b_w_out, b_sink, c_w_in, c_w_out, f_w_gu, f_w_down, moe_router,
           moe_w_gu, moe_w_down, final_norm):
    bp, seq, d = x_prompt.shape
    bs = x_sample.shape[0]
    assert x_sample.shape[1] == seq
    nb = bp + bs
    x = jnp.concatenate([x_prompt.reshape(bp * seq, d), x_sample.reshape(bs * seq, d)], axis=0)
    c8 = jnp.concatenate([c_prompt, c_sample, jnp.zeros((8 - nb, d), F32)], axis=0)
    y = trunk_all(x, c8, nb, seq, ada_w, ada_b, norm_mix, norm_ffn, a_w_in, a_w_out, a_lambda,
                  a_subln, b_w_in, b_w_out, b_sink, c_w_in, c_w_out, f_w_gu, f_w_down, moe_router,
                  moe_w_gu, moe_w_down, final_norm)
    return (y[:bp * seq].reshape(bp, seq, d), y[bp * seq:].reshape(bs, seq, d))
```

```python
import functools
import math

import jax
import jax.numpy as jnp
from jax import lax
from jax.experimental import pallas as pl
from jax.experimental.pallas import tpu as pltpu

F32 = jnp.float32
BF16 = jnp.bfloat16

D_MODEL = 2048
DEPTH = 4
HEAD_DIM = 128
ROPE_THETA = 500000.0
ROT = HEAD_DIM // 4
NORM_EPS = 1e-6
NEG = -1e30
A_HEADS = 8
B_HEADS = 16
B_KV_HEADS = 4
B_WINDOW = 128
C_CONFIGS = ((128, 1), (512, 4), (2048, 16))
C_HEADS = 8
C_WIDTH = C_HEADS * HEAD_DIM
FFN_DIM = 2 * D_MODEL
MOE_DIM = D_MODEL // 2
N_EXPERTS = 8

LOG2E = math.log2(math.e)
Q_SCALE = HEAD_DIM ** -0.5 * LOG2E

LANES = 128
VMEM_LIMIT = 56 * 1024 * 1024

NT_DIMS = (((1,), (1,)), ((), ()))


def _params(sem, vmem=VMEM_LIMIT):
    return pltpu.CompilerParams(dimension_semantics=sem, vmem_limit_bytes=vmem)


def _dot(a, b):
    return jnp.dot(a, b, preferred_element_type=F32)


def _split_bf16(x):
    hi = x.astype(BF16)
    lo = (x - hi.astype(F32)).astype(BF16)
    return hi, lo


def _dot3(a, b):
    a_hi, a_lo = _split_bf16(a)
    b_hi, b_lo = _split_bf16(b)
    return _dot(a_hi, b_hi) + (_dot(a_hi, b_lo) + _dot(a_lo, b_hi))


def _norm_mod(x, g, sc, sh):
    ms = jnp.mean(x * x, axis=-1, keepdims=True)
    y = x * lax.rsqrt(ms + NORM_EPS)
    return (y * g) * (1.0 + sc) + sh


def _adaln_kernel(c_ref, w_ref, b_ref, o_ref):
    c = c_ref[...]
    cs = c * jax.nn.sigmoid(c)
    o_ref[...] = _dot3(cs, w_ref[...]) + b_ref[...]


def adaln(c8, ada_w, ada_b, tn=512):
    depth, d, n = ada_w.shape
    return pl.pallas_call(
        _adaln_kernel,
        out_shape=jax.ShapeDtypeStruct((depth, 8, n), F32),
        grid=(depth, n // tn),
        in_specs=[
            pl.BlockSpec((8, d), lambda l, j: (0, 0)),
            pl.BlockSpec((None, d, tn), lambda l, j: (l, 0, j)),
            pl.BlockSpec((None, 1, tn), lambda l, j: (l, 0, j)),
        ],
        out_specs=pl.BlockSpec((None, 8, tn), lambda l, j: (l, 0, j)),
        compiler_params=_params(("arbitrary", "arbitrary")),
        name="adaln",
    )(c8, ada_w, ada_b.reshape(depth, 1, n))


def _rope_tables(seq):
    pos = jnp.arange(seq, dtype=F32)
    half = ROT // 2
    inv = ROPE_THETA ** (-jnp.arange(half, dtype=F32) * (2.0 / ROT))
    ang = pos[:, None] * inv[None, :]
    cos, sin = jnp.cos(ang), jnp.sin(ang)
    zeros = lambda n: jnp.zeros((seq, n), F32)
    cos_t = jnp.concatenate([cos, cos, jnp.ones((seq, HEAD_DIM - ROT), F32)], axis=1)
    sin_a = jnp.concatenate([zeros(half), sin, zeros(HEAD_DIM - ROT)], axis=1)
    sin_b = jnp.concatenate([-sin, zeros(HEAD_DIM - half)], axis=1)
    return cos_t, sin_a, sin_b


def _perm_tables(tables, dil):
    return tuple(t.reshape(-1, dil, HEAD_DIM).transpose(1, 0, 2) for t in tables)


def _inproj_kernel(x_ref, g_ref, sc_ref, sh_ref, w_ref, cos_ref, sa_ref, sb_ref, o_ref, h_scr,
                   *maybe_slab_scr, n_q, n_rope, tn, dil):
    j = pl.program_id(1)
    tm, d = x_ref.shape
    n = tm // dil

    @pl.when(j == 0)
    def _():
        h = _norm_mod(x_ref[...], g_ref[...], sc_ref[...], sh_ref[...])
        if dil == 1:
            h_scr[...] = h.astype(BF16)
        else:
            (slab_scr,) = maybe_slab_scr
            for c in range(d // LANES):
                slab_scr[c] = h[:, c * LANES:(c + 1) * LANES]
            for r in range(dil):
                for c in range(d // LANES):
                    rows = slab_scr[c, pl.ds(r, n, stride=dil), :]
                    h_scr[r * n:(r + 1) * n, c * LANES:(c + 1) * LANES] = rows.astype(BF16)

    acc = _dot(h_scr[...], w_ref[...])
    is_rope = j < n_rope
    scale = jnp.where(j < n_q, Q_SCALE, 1.0).astype(F32)
    cos = jnp.where(is_rope, cos_ref[...].reshape(tm, HEAD_DIM) * scale, 1.0)
    sa = jnp.where(is_rope, sa_ref[...].reshape(tm, HEAD_DIM) * scale, 0.0)
    sb = jnp.where(is_rope, sb_ref[...].reshape(tm, HEAD_DIM) * scale, 0.0)
    for hh in range(tn // HEAD_DIM):
        a = acc[:, hh * HEAD_DIM:(hh + 1) * HEAD_DIM]
        r = a * cos + pltpu.roll(a, ROT // 2, 1) * sa + pltpu.roll(a, HEAD_DIM - ROT // 2, 1) * sb
        o_ref[:, :, hh * HEAD_DIM:(hh + 1) * HEAD_DIM] = r.reshape(dil, n, HEAD_DIM).astype(BF16)


def inproj(x, mod3, g, w, tables, seq, *, q_cols, rope_cols, tm, tn, dil=1, col_stride=1, col_off=0,
           n_cols=None):
    t, d = x.shape
    n_cols = n_cols or w.shape[1]
    tps = seq // tm
    n = tm // dil
    kern = functools.partial(_inproj_kernel, n_q=q_cols // tn, n_rope=rope_cols // tn, tn=tn, dil=dil)
    mod_spec = lambda k: pl.BlockSpec((None, 1, d), lambda i, j: (i // tps, 0, k))
    tab_spec = pl.BlockSpec((dil, n, HEAD_DIM), lambda i, j: (0, i % tps, 0))
    return pl.pallas_call(
        kern,
        out_shape=jax.ShapeDtypeStruct((t // seq, dil, seq // dil, n_cols), BF16),
        grid=(t // tm, n_cols // tn),
        in_specs=[
            pl.BlockSpec((tm, d), lambda i, j: (i, 0)),
            pl.BlockSpec((1, d), lambda i, j: (0, 0)),
            mod_spec(1), mod_spec(0),
            pl.BlockSpec((d, tn), lambda i, j: (0, j * col_stride + col_off)),
            tab_spec, tab_spec, tab_spec,
        ],
        out_specs=pl.BlockSpec((None, dil, n, tn), lambda i, j: (i // tps, 0, i % tps, j)),
        scratch_shapes=[pltpu.VMEM((tm, d), BF16)]
        + ([pltpu.VMEM((d // LANES, tm, LANES), F32)] if dil > 1 else []),
        compiler_params=_params(("parallel", "arbitrary")),
        name="inproj",
    )(x, g.reshape(1, d), mod3, mod3, w, *_perm_tables(tables, dil))


L_SAFE = 2.0 ** -64


def _attn_a_kernel(q_ref, k_ref, v_ref, lam_ref, subln_ref, o_ref, p_scr, knorm_scr,
                   *, tk, tk_pv, lam0, unroll):
    i = pl.program_id(2)
    seq = k_ref.shape[0]
    tq = q_ref.shape[0]
    dv = v_ref.shape[1]
    n_chunks = seq // tk

    @pl.when(i == 0)
    def _():
        def body(c, carry):
            off = pl.multiple_of(c * tk, tk)
            kf = k_ref[pl.ds(off, tk), :].astype(F32)
            sq = kf * kf
            n0 = jnp.max(jnp.sum(sq[:, :HEAD_DIM], axis=1, keepdims=True), axis=0, keepdims=True)
            n1 = jnp.max(jnp.sum(sq[:, HEAD_DIM:], axis=1, keepdims=True), axis=0, keepdims=True)
            return jnp.maximum(carry[0], n0), jnp.maximum(carry[1], n1)

        zero = jnp.zeros((1, 1), F32)
        n0, n1 = lax.fori_loop(0, n_chunks, body, (zero, zero))
        knorm_scr[0:1, :] = jnp.broadcast_to(jnp.sqrt(n0), (1, LANES))
        knorm_scr[1:2, :] = jnp.broadcast_to(jnp.sqrt(n1), (1, LANES))

    q = q_ref[...]
    qs = (q[:, :HEAD_DIM], q[:, HEAD_DIM:])
    lp = lam_ref[...].astype(F32)
    lam = (jnp.exp(jnp.sum(lp[0:1] * lp[1:2], axis=1, keepdims=True))
           - jnp.exp(jnp.sum(lp[2:3] * lp[3:4], axis=1, keepdims=True)) + lam0)

    def finish(o):
        ms = jnp.mean(o * o, axis=1, keepdims=True)
        y = (o * lax.rsqrt(ms + NORM_EPS)) * subln_ref[...]
        o_ref[...] = (y * (1.0 - lam0)).astype(BF16)

    ls = []
    for mp in range(2):
        qc = qs[mp]
        qf = qc.astype(F32)
        qn = jnp.sqrt(jnp.sum(qf * qf, axis=1, keepdims=True))
        shift = qn * (knorm_scr[mp:mp + 1, :] * 1.001) + 1.0
        cols = slice(mp * HEAD_DIM, (mp + 1) * HEAD_DIM)

        def chunk(c, lpart, mp=mp, qc=qc, shift=shift, cols=cols):
            off = pl.multiple_of(c * tk, tk)
            s = lax.dot_general(qc, k_ref[pl.ds(off, tk), cols], NT_DIMS, preferred_element_type=F32)
            for g in range(tk // LANES):
                p = jnp.exp2(s[:, g * LANES:(g + 1) * LANES] - shift)
                lpart = lpart + p
                p_scr[mp, :, pl.ds(pl.multiple_of(off + g * LANES, LANES), LANES)] = p
            return lpart

        lpart = lax.fori_loop(0, n_chunks, chunk, jnp.zeros((tq, LANES), F32), unroll=unroll)
        ls.append(jnp.sum(lpart, axis=1, keepdims=True))

    l_min = jnp.min(jnp.minimum(ls[0], ls[1]))

    @pl.when(l_min >= L_SAFE)
    def _():
        ratio = jnp.broadcast_to(lam * ls[0] / ls[1], (tq, LANES))
        acc = jnp.zeros((tq, dv), F32)
        for c in range(seq // tk_pv):
            groups = []
            for g in range(tk_pv // LANES):
                cs = slice(c * tk_pv + g * LANES, c * tk_pv + (g + 1) * LANES)
                groups.append((p_scr[0, :, cs] - ratio * p_scr[1, :, cs]).astype(BF16))
            a = jnp.concatenate(groups, axis=1)
            acc = acc + _dot(a, v_ref[c * tk_pv:(c + 1) * tk_pv, :])
        finish(acc / ls[0])

    @pl.when(l_min < L_SAFE)
    def _():
        def update(qc, kc, v, m, l, acc):
            s = lax.dot_general(qc, kc, NT_DIMS, preferred_element_type=F32)
            m_new = jnp.maximum(m, jnp.max(s, axis=1, keepdims=True))
            alpha = jnp.exp2(m - m_new)
            p = jnp.exp2(s - m_new)
            l = alpha * l + jnp.sum(p, axis=1, keepdims=True)
            acc = alpha * acc + _dot(p.astype(BF16), v)
            return m_new, l, acc

        def body(c, carry):
            m0, l0, a0, m1, l1, a1 = carry
            off = pl.multiple_of(c * tk, tk)
            k = k_ref[pl.ds(off, tk), :]
            v = v_ref[pl.ds(off, tk), :]
            m0, l0, a0 = update(qs[0], k[:, :HEAD_DIM], v, m0, l0, a0)
            m1, l1, a1 = update(qs[1], k[:, HEAD_DIM:], v, m1, l1, a1)
            return m0, l0, a0, m1, l1, a1

        m_init = jnp.full((tq, 1), NEG, F32)
        l_init = jnp.zeros((tq, 1), F32)
        a_init = jnp.zeros((tq, dv), F32)
        _, l0, a0, _, l1, a1 = lax.fori_loop(
            0, n_chunks, body, (m_init, l_init, a_init, m_init, l_init, a_init))
        finish(a0 / l0 - lam * (a1 / l1))


def attn_a(qkv, lam_p, subln, nb, seq, lam0, *, tq, tk, tk_pv, unroll):
    t = qkv.shape[0]
    dv = 2 * HEAD_DIM
    nq = seq // tq
    kern = functools.partial(_attn_a_kernel, tk=tk, tk_pv=tk_pv, lam0=lam0, unroll=unroll)
    return pl.pallas_call(
        kern,
        out_shape=jax.ShapeDtypeStruct((t, A_HEADS * dv), BF16),
        grid=(nb, A_HEADS, nq),
        in_specs=[
            pl.BlockSpec((tq, dv), lambda b, h, i: (b * nq + i, h)),
            pl.BlockSpec((seq, dv), lambda b, h, i: (b, A_HEADS + h)),
            pl.BlockSpec((seq, dv), lambda b, h, i: (b, 2 * A_HEADS + h)),
            pl.BlockSpec((4, HEAD_DIM), lambda b, h, i: (0, 0)),
            pl.BlockSpec((1, dv), lambda b, h, i: (0, 0)),
        ],
        out_specs=pl.BlockSpec((tq, dv), lambda b, h, i: (b * nq + i, h)),
        scratch_shapes=[pltpu.VMEM((2, tq, seq), F32), pltpu.VMEM((8, LANES), F32)],
        compiler_params=_params(("parallel", "parallel", "arbitrary")),
        name="attn_a",
    )(qkv, qkv, qkv, lam_p, subln.reshape(1, dv))


def _band_kernel(*refs, half, tl, seq_len, n_kv, group, has_sink, emit_lse):
    if has_sink:
        sink_ref, refs = refs[0], refs[1:]
    q_ref, kp_ref, kc_ref, kn_ref, vp_ref, vc_ref, vn_ref = refs[:7]
    o_ref = refs[7]
    lse_ref = refs[8] if emit_lse else None
    i = pl.program_id(1)
    win = tl + 2 * half
    qpos = i * tl + lax.broadcasted_iota(jnp.int32, (tl, win), 0)
    kpos = i * tl - half + lax.broadcasted_iota(jnp.int32, (tl, win), 1)
    dist = jnp.abs(kpos - qpos)
    dist = jnp.where(kpos < 0, win, dist)
    dist = jnp.where(kpos >= seq_len, win, dist)
    valid = dist <= half
    for kv in range(n_kv):
        cs = slice(kv * HEAD_DIM, (kv + 1) * HEAD_DIM)
        kw = jnp.concatenate([kp_ref[:, cs], kc_ref[:, cs], kn_ref[:, cs]], axis=0)
        vw = jnp.concatenate([vp_ref[:, cs], vc_ref[:, cs], vn_ref[:, cs]], axis=0)
        for g in range(group):
            hd = kv * group + g
            hs = slice(hd * HEAD_DIM, (hd + 1) * HEAD_DIM)
            s = lax.dot_general(q_ref[:, hs], kw, NT_DIMS, preferred_element_type=F32)
            s = jnp.where(valid, s, NEG)
            m = jnp.max(s, axis=1, keepdims=True)
            if has_sink:
                sk = sink_ref[hd] * LOG2E
                m = jnp.maximum(m, sk)
            e = jnp.exp2(s - m)
            den = jnp.sum(e, axis=1, keepdims=True)
            if has_sink:
                den = den + jnp.exp2(sk - m)
            o = _dot(e.astype(BF16), vw) / den
            o_ref[:, hs] = o.astype(o_ref.dtype)
            if emit_lse:
                lse_ref[:, hs] = jnp.broadcast_to(m + jnp.log2(den), (tl, HEAD_DIM))


def band_attn(qkv, *, q_col, k_col, v_col, n_kv, group, half, tl, sink=None, emit_lse=False):
    nb, seq_len, _ = qkv.shape
    wq = n_kv * group * HEAD_DIM
    wk = n_kv * HEAD_DIM
    nq = seq_len // tl
    hb = tl // half
    nh = seq_len // half
    kern = functools.partial(_band_kernel, half=half, tl=tl, seq_len=seq_len, n_kv=n_kv, group=group,
                             has_sink=sink is not None, emit_lse=emit_lse)

    def halo(col):
        prev = pl.BlockSpec((None, half, wk), lambda b, i: (b, jnp.maximum(i * hb - 1, 0), col))
        cur = pl.BlockSpec((None, tl, wk), lambda b, i: (b, i, col))
        nxt = pl.BlockSpec((None, half, wk), lambda b, i: (b, jnp.minimum((i + 1) * hb, nh - 1), col))
        return [prev, cur, nxt]

    in_specs = [pl.BlockSpec((None, tl, wq), lambda b, i: (b, i, q_col))] + halo(k_col) + halo(v_col)
    args = [qkv] * 7
    if sink is not None:
        in_specs = [pl.BlockSpec(memory_space=pltpu.SMEM)] + in_specs
        args = [sink] + args
    out_spec = pl.BlockSpec((None, tl, wq), lambda b, i: (b, i, 0))
    o_shape = jax.ShapeDtypeStruct((nb, seq_len, wq), BF16)
    if emit_lse:
        out_shape = (o_shape, jax.ShapeDtypeStruct((nb, seq_len, wq), F32))
        out_specs = (out_spec, out_spec)
    else:
        out_shape, out_specs = o_shape, out_spec
    return pl.pallas_call(
        kern,
        out_shape=out_shape,
        grid=(nb, nq),
        in_specs=in_specs,
        out_specs=out_specs,
        compiler_params=_params(("parallel", "arbitrary")),
        name="band_attn",
    )(*args)


def _outproj_kernel(a_ref, w_ref, x_ref, gate_ref, o_ref):
    o_ref[...] = x_ref[...] + gate_ref[...] * _dot(a_ref[...], w_ref[...])


def outproj(a, w, x, mod3, gate_chunk, seq, *, tm, tn):
    t, k = a.shape
    d = w.shape[1]
    tps = seq // tm
    nj = d // tn
    return pl.pallas_call(
        _outproj_kernel,
        out_shape=jax.ShapeDtypeStruct((t, d), F32),
        grid=(t // tm, nj),
        in_specs=[
            pl.BlockSpec((tm, k), lambda i, j: (i, 0)),
            pl.BlockSpec((k, tn), lambda i, j: (0, j)),
            pl.BlockSpec((tm, tn), lambda i, j: (i, j)),
            pl.BlockSpec((None, 1, tn), lambda i, j: (i // tps, 0, gate_chunk * nj + j)),
        ],
        out_specs=pl.BlockSpec((tm, tn), lambda i, j: (i, j)),
        compiler_params=_params(("parallel", "arbitrary")),
        name="outproj",
    )(a, w, x, mod3)


def _merge_outproj_kernel(*refs, dils):
    n_g = len(dils)
    o_refs, l_refs = refs[:n_g], refs[n_g:2 * n_g]
    w_ref, x_ref, gate_ref, o_ref, a_scr = refs[2 * n_g:2 * n_g + 5]
    un_scr = refs[2 * n_g + 5:]
    j = pl.program_id(1)

    @pl.when(j == 0)
    def _():
        os_, ls = [], []
        si = 0
        for og_ref, lg_ref, dil in zip(o_refs, l_refs, dils):
            if dil == 1:
                os_.append(og_ref[0].astype(F32))
                ls.append(lg_ref[0])
            else:
                so, sl = un_scr[si], un_scr[si + 1]
                si += 2
                n, k = og_ref.shape[1:]
                for r in range(dil):
                    for c in range(k // LANES):
                        cols = slice(c * LANES, (c + 1) * LANES)
                        so[c, pl.ds(r, n, stride=dil), :] = og_ref[r, :, cols].astype(F32)
                        sl[c, pl.ds(r, n, stride=dil), :] = lg_ref[r, :, cols]
                os_.append(jnp.concatenate([so[c] for c in range(k // LANES)], axis=1))
                ls.append(jnp.concatenate([sl[c] for c in range(k // LANES)], axis=1))
        m = functools.reduce(jnp.maximum, ls)
        es = [jnp.exp2(l - m) for l in ls]
        den = functools.reduce(lambda a, b: a + b, es)
        merged = functools.reduce(lambda a, b: a + b, [(e / den) * o for e, o in zip(es, os_)])
        a_scr[...] = merged.astype(BF16)

    o_ref[...] = x_ref[...] + gate_ref[...] * _dot(a_scr[...], w_ref[...])


def merge_outproj(os_, lses, dils, w, x, mod3, gate_chunk, seq, *, tm, tn):
    k = os_[0].shape[-1]
    t, d = x.shape
    tps = seq // tm
    nj = d // tn
    grp = lambda dil: pl.BlockSpec((None, dil, tm // dil, k), lambda i, j: (i // tps, 0, i % tps, 0))
    n_un = 2 * sum(dil != 1 for dil in dils)
    kern = functools.partial(_merge_outproj_kernel, dils=tuple(dils))
    return pl.pallas_call(
        kern,
        out_shape=jax.ShapeDtypeStruct((t, d), F32),
        grid=(t // tm, nj),
        in_specs=[grp(dil) for dil in dils] * 2 + [
            pl.BlockSpec((k, tn), lambda i, j: (0, j)),
            pl.BlockSpec((tm, tn), lambda i, j: (i, j)),
            pl.BlockSpec((None, 1, tn), lambda i, j: (i // tps, 0, gate_chunk * nj + j)),
        ],
        out_specs=pl.BlockSpec((tm, tn), lambda i, j: (i, j)),
        scratch_shapes=[pltpu.VMEM((tm, k), BF16)] + [pltpu.VMEM((k // LANES, tm, LANES), F32)] * n_un,
        compiler_params=_params(("parallel", "arbitrary")),
        name="merge_outproj",
    )(*os_, *lses, w, x, mod3)


def _silu_mul(g, u):
    return (g * jax.nn.sigmoid(g)) * u


def _ffn_up_kernel(x_ref, g_ref, sc_ref, sh_ref, wg_ref, wu_ref, o_ref, h_scr):
    j = pl.program_id(1)

    @pl.when(j == 0)
    def _():
        h_scr[...] = _norm_mod(x_ref[...], g_ref[...], sc_ref[...], sh_ref[...]).astype(BF16)

    h = h_scr[...]
    o_ref[...] = _silu_mul(_dot(h, wg_ref[...]), _dot(h, wu_ref[...])).astype(BF16)


def ffn_up(x, mod3, g, w_gu, seq, *, tm, tn):
    t, d = x.shape
    f = w_gu.shape[1] // 2
    tps = seq // tm
    nj = f // tn
    mod_spec = lambda k: pl.BlockSpec((None, 1, d), lambda i, j: (i // tps, 0, k))
    return pl.pallas_call(
        _ffn_up_kernel,
        out_shape=jax.ShapeDtypeStruct((t, f), BF16),
        grid=(t // tm, nj),
        in_specs=[
            pl.BlockSpec((tm, d), lambda i, j: (i, 0)),
            pl.BlockSpec((1, d), lambda i, j: (0, 0)),
            mod_spec(4), mod_spec(3),
            pl.BlockSpec((d, tn), lambda i, j: (0, j)),
            pl.BlockSpec((d, tn), lambda i, j: (0, nj + j)),
        ],
        out_specs=pl.BlockSpec((tm, tn), lambda i, j: (i, j)),
        scratch_shapes=[pltpu.VMEM((tm, d), BF16)],
        compiler_params=_params(("parallel", "arbitrary")),
        name="ffn_up",
    )(x, g.reshape(1, d), mod3, mod3, w_gu, w_gu)


def _top2_gates(logits):
    lane = lax.broadcasted_iota(jnp.int32, logits.shape, 1).astype(F32)
    logits = jnp.where(lane < N_EXPERTS, logits, NEG)
    v1 = jnp.max(logits, axis=1, keepdims=True)
    i1 = jnp.min(jnp.where(logits == v1, lane, float(LANES)), axis=1, keepdims=True)
    rest = jnp.where(lane == i1, NEG, logits)
    v2 = jnp.max(rest, axis=1, keepdims=True)
    i2 = jnp.min(jnp.where(rest == v2, lane, float(LANES)), axis=1, keepdims=True)
    e2 = jnp.exp(v2 - v1)
    w1 = 1.0 / (1.0 + e2)
    w2 = e2 / (1.0 + e2)
    return jnp.where(lane == i1, w1, 0.0) + jnp.where(lane == i2, w2, 0.0)


def _moe_up_kernel(x_ref, g_ref, sc_ref, sh_ref, r_ref, wg_ref, wu_ref, o_ref, h_scr, gate_scr,
                   *, tiles_per_expert):
    j = pl.program_id(1)

    @pl.when(j == 0)
    def _():
        h = _norm_mod(x_ref[...], g_ref[...], sc_ref[...], sh_ref[...])
        h_scr[...] = h.astype(BF16)
        gates = _top2_gates(_dot3(h, r_ref[...]))
        lane = lax.broadcasted_iota(jnp.int32, gates.shape, 1)
        for e in range(N_EXPERTS):
            col = jnp.sum(jnp.where(lane == e, gates, 0.0), axis=1, keepdims=True)
            gate_scr[e] = jnp.broadcast_to(col, gates.shape)

    h = h_scr[...]
    act = _silu_mul(_dot(h, wg_ref[...]), _dot(h, wu_ref[...]))
    gate = gate_scr[j // tiles_per_expert]
    tn = act.shape[1]
    for c in range(tn // LANES):
        o_ref[:, c * LANES:(c + 1) * LANES] = (act[:, c * LANES:(c + 1) * LANES] * gate).astype(BF16)


def moe_up(x, mod3, g, router, w_gu, seq, *, tm, tn):
    t, d = x.shape
    n_e, _, two_f = w_gu.shape
    f = two_f // 2
    tps = seq // tm
    tpe = f // tn
    mod_spec = lambda k: pl.BlockSpec((None, 1, d), lambda i, j: (i // tps, 0, k))
    router_p = jnp.pad(router, ((0, 0), (0, LANES - n_e)))
    kern = functools.partial(_moe_up_kernel, tiles_per_expert=tpe)
    return pl.pallas_call(
        kern,
        out_shape=jax.ShapeDtypeStruct((t, n_e * f), BF16),
        grid=(t // tm, n_e * tpe),
        in_specs=[
            pl.BlockSpec((tm, d), lambda i, j: (i, 0)),
            pl.BlockSpec((1, d), lambda i, j: (0, 0)),
            mod_spec(4), mod_spec(3),
            pl.BlockSpec((d, LANES), lambda i, j: (0, 0)),
            pl.BlockSpec((None, d, tn), lambda i, j: (j // tpe, 0, j % tpe)),
            pl.BlockSpec((None, d, tn), lambda i, j: (j // tpe, 0, tpe + j % tpe)),
        ],
        out_specs=pl.BlockSpec((tm, tn), lambda i, j: (i, j)),
        scratch_shapes=[pltpu.VMEM((tm, d), BF16), pltpu.VMEM((N_EXPERTS, tm, LANES), F32)],
        compiler_params=_params(("parallel", "arbitrary")),
        name="moe_up",
    )(x, g.reshape(1, d), mod3, mod3, router_p, w_gu, w_gu)


def _final_norm_kernel(x_ref, g_ref, o_ref):
    x = x_ref[...]
    ms = jnp.mean(x * x, axis=-1, keepdims=True)
    o_ref[...] = (x * lax.rsqrt(ms + NORM_EPS)) * g_ref[...]


def final_norm(x, g, *, tm):
    t, d = x.shape
    return pl.pallas_call(
        _final_norm_kernel,
        out_shape=jax.ShapeDtypeStruct((t, d), F32),
        grid=(t // tm,),
        in_specs=[pl.BlockSpec((tm, d), lambda i: (i, 0)), pl.BlockSpec((1, d), lambda i: (0, 0))],
        out_specs=pl.BlockSpec((tm, d), lambda i: (i, 0)),
        compiler_params=_params(("parallel",)),
        name="final_norm",
    )(x, g.reshape(1, d))


def _lambda_init(layer):
    return 0.8 - 0.6 * math.exp(-0.3 * layer)


def _mixer_a(x, mod3, g, w_in, w_out, lam_p, subln, tables, nb, seq, layer, cfg):
    qkv = inproj(x, mod3, g, w_in, tables, seq, q_cols=D_MODEL, rope_cols=2 * D_MODEL,
                 tm=cfg["tm"], tn=cfg["tn_in"]).reshape(nb * seq, 3 * D_MODEL)
    o = attn_a(qkv, lam_p, subln, nb, seq, _lambda_init(layer), tq=cfg["tq_a"], tk=cfg["tk_a"],
               tk_pv=cfg["tk_pv_a"], unroll=cfg["unroll_a"])
    return outproj(o, w_out, x, mod3, 2, seq, tm=cfg["tm"], tn=cfg["tn_out"])


def _mixer_b(x, mod3, g, w_in, w_out, sink, tables, nb, seq, cfg):
    nq = B_HEADS * HEAD_DIM
    qkv = inproj(x, mod3, g, w_in, tables, seq, q_cols=nq, rope_cols=nq + B_KV_HEADS * HEAD_DIM,
                 tm=cfg["tm"], tn=cfg["tn_in_b"]).reshape(nb, seq, -1)
    o = band_attn(qkv, q_col=0, k_col=4, v_col=5, n_kv=B_KV_HEADS, group=B_HEADS // B_KV_HEADS,
                  half=B_WINDOW, tl=cfg["tl_b"], sink=sink)
    return outproj(o.reshape(nb * seq, nq), w_out, x, mod3, 2, seq, tm=cfg["tm"], tn=cfg["tn_out"])


def _mixer_c(x, mod3, g, w_in, w_out, tables, nb, seq, cfg):
    n_groups = len(C_CONFIGS)
    os_, lses, dils = [], [], []
    for gi, (window, dil) in enumerate(C_CONFIGS):
        qkv = inproj(x, mod3, g, w_in, tables, seq, q_cols=C_WIDTH, rope_cols=2 * C_WIDTH,
                     tm=cfg["tm"], tn=C_WIDTH, dil=dil, col_stride=n_groups, col_off=gi,
                     n_cols=3 * C_WIDTH)
        sub = seq // dil
        o, lse = band_attn(qkv.reshape(nb * dil, sub, 3 * C_WIDTH), q_col=0, k_col=1, v_col=2,
                           n_kv=C_HEADS, group=1, half=window // (2 * dil), tl=cfg["tl_c"],
                           emit_lse=True)
        os_.append(o.reshape(nb, dil, sub, C_WIDTH))
        lses.append(lse.reshape(nb, dil, sub, C_WIDTH))
        dils.append(dil)
    return merge_outproj(os_, lses, dils, w_out, x, mod3, 2, seq, tm=cfg["tm_merge"], tn=cfg["tn_out"])


def _default_cfg(seq):
    return dict(tm=min(512, seq), tn_in=1024, tn_in_b=512, tn_out=512, tq_a=min(256, seq),
                tk_a=min(512, seq), tl_b=min(256, seq), tl_c=min(256, seq // C_CONFIGS[-1][1]),
                tm_merge=min(512, seq), tn_up=512, tm_down=min(512, seq), unroll_a=True,
                tk_pv_a=min(2048, seq))


def trunk_all(x, c8, nb, seq, ada_w, ada_b, norm_mix, norm_ffn, a_w_in, a_w_out, a_lambda, a_subln,
              b_w_in, b_w_out, b_sink, c_w_in, c_w_out, f_w_gu, f_w_down, moe_router, moe_w_gu,
              moe_w_down, final_norm_g, cfg=None):
    cfg = cfg or _default_cfg(seq)
    bf = lambda w: w.astype(BF16)
    mod_all = adaln(c8, ada_w, ada_b)
    tables = _rope_tables(seq)
    for i in range(DEPTH):
        mod3 = mod_all[i].reshape(8, 1, 6 * D_MODEL)
        kind, jm = i % 3, i // 3
        if kind == 0:
            x = _mixer_a(x, mod3, norm_mix[i], bf(a_w_in[jm]), bf(a_w_out[jm]), a_lambda[jm],
                         a_subln[jm], tables, nb, seq, i, cfg)
        elif kind == 1:
            x = _mixer_b(x, mod3, norm_mix[i], bf(b_w_in[jm]), bf(b_w_out[jm]), b_sink[jm], tables,
                         nb, seq, cfg)
        else:
            x = _mixer_c(x, mod3, norm_mix[i], bf(c_w_in[jm]), bf(c_w_out[jm]), tables, nb, seq, cfg)
        jf = i // 2
        if i % 2 == 0:
            act = ffn_up(x, mod3, norm_ffn[i], bf(f_w_gu[jf]), seq, tm=cfg["tm"], tn=cfg["tn_up"])
            x = outproj(act, bf(f_w_down[jf]), x, mod3, 5, seq, tm=cfg["tm"], tn=cfg["tn_out"])
        else:
            act = moe_up(x, mod3, norm_ffn[i], moe_router[jf], bf(moe_w_gu[jf]), seq,
                         tm=cfg["tm"], tn=cfg["tn_up"])
            w_down = bf(moe_w_down[jf]).reshape(N_EXPERTS * MOE_DIM, D_MODEL)
            x = outproj(act, w_down, x, mod3, 5, seq, tm=cfg["tm_down"], tn=cfg["tn_out"])
    return final_norm(x, final_norm_g, tm=cfg["tm_down"])


def kernel(x_prompt, x_sample, c_prompt, c_sample, ada_w, ada_b, norm_mix, norm_ffn, a_w_in, a_w_out,
           a_lambda, a_subln, b_w_in, b_w_out, b_sink, c_w_in, c_w_out, f_w_gu, f_w_down, moe_router,
           moe_w_gu, moe_w_down, final_norm):
    bp, seq, d = x_prompt.shape
    bs = x_sample.shape[0]
    assert x_sample.shape[1] == seq
    nb = bp + bs
    x = jnp.concatenate([x_prompt.reshape(bp * seq, d), x_sample.reshape(bs * seq, d)], axis=0)
    c8 = jnp.concatenate([c_prompt, c_sample, jnp.zeros((8 - nb, d), F32)], axis=0)
    y = trunk_all(x, c8, nb, seq, ada_w, ada_b, norm_mix, norm_ffn, a_w_in, a_w_out, a_lambda,
                  a_subln, b_w_in, b_w_out, b_sink, c_w_in, c_w_out, f_w_gu, f_w_down, moe_router,
                  moe_w_gu, moe_w_down, final_norm)
    return (y[:bp * seq].reshape(bp, seq, d), y[bp * seq:].reshape(bs, seq, d))
```

```python
import functools
import math

import jax
import jax.numpy as jnp
from jax import lax
from jax.experimental import pallas as pl
from jax.experimental.pallas import tpu as pltpu

F32 = jnp.float32
BF16 = jnp.bfloat16

D_MODEL = 2048
DEPTH = 4
HEAD_DIM = 128
ROPE_THETA = 500000.0
ROT = HEAD_DIM // 4
NORM_EPS = 1e-6
NEG = -1e30
A_HEADS = 8
B_HEADS = 16
B_KV_HEADS = 4
B_WINDOW = 128
C_CONFIGS = ((128, 1), (512, 4), (2048, 16))
C_HEADS = 8
C_WIDTH = C_HEADS * HEAD_DIM
FFN_DIM = 2 * D_MODEL
MOE_DIM = D_MODEL // 2
N_EXPERTS = 8

LOG2E = math.log2(math.e)
Q_SCALE = HEAD_DIM ** -0.5 * LOG2E

LANES = 128
VMEM_LIMIT = 56 * 1024 * 1024

NT_DIMS = (((1,), (1,)), ((), ()))


def _params(sem, vmem=VMEM_LIMIT):
    return pltpu.CompilerParams(dimension_semantics=sem, vmem_limit_bytes=vmem)


def _dot(a, b):
    return jnp.dot(a, b, preferred_element_type=F32)


def _split_bf16(x):
    hi = x.astype(BF16)
    lo = (x - hi.astype(F32)).astype(BF16)
    return hi, lo


def _dot3(a, b):
    a_hi, a_lo = _split_bf16(a)
    b_hi, b_lo = _split_bf16(b)
    return _dot(a_hi, b_hi) + (_dot(a_hi, b_lo) + _dot(a_lo, b_hi))


def _norm_mod(x, g, sc, sh):
    ms = jnp.mean(x * x, axis=-1, keepdims=True)
    y = x * lax.rsqrt(ms + NORM_EPS)
    return (y * g) * (1.0 + sc) + sh


def _adaln_kernel(c_ref, w_ref, b_ref, o_ref):
    c = c_ref[...]
    cs = c * jax.nn.sigmoid(c)
    o_ref[...] = _dot3(cs, w_ref[...]) + b_ref[...]


def adaln(c8, ada_w, ada_b, tn=512):
    depth, d, n = ada_w.shape
    return pl.pallas_call(
        _adaln_kernel,
        out_shape=jax.ShapeDtypeStruct((depth, 8, n), F32),
        grid=(depth, n // tn),
        in_specs=[
            pl.BlockSpec((8, d), lambda l, j: (0, 0)),
            pl.BlockSpec((None, d, tn), lambda l, j: (l, 0, j)),
            pl.BlockSpec((None, 1, tn), lambda l, j: (l, 0, j)),
        ],
        out_specs=pl.BlockSpec((None, 8, tn), lambda l, j: (l, 0, j)),
        compiler_params=_params(("arbitrary", "arbitrary")),
        name="adaln",
    )(c8, ada_w, ada_b.reshape(depth, 1, n))


def _rope_tables(seq):
    pos = jnp.arange(seq, dtype=F32)
    half = ROT // 2
    inv = ROPE_THETA ** (-jnp.arange(half, dtype=F32) * (2.0 / ROT))
    ang = pos[:, None] * inv[None, :]
    cos, sin = jnp.cos(ang), jnp.sin(ang)
    zeros = lambda n: jnp.zeros((seq, n), F32)
    cos_t = jnp.concatenate([cos, cos, jnp.ones((seq, HEAD_DIM - ROT), F32)], axis=1)
    sin_a = jnp.concatenate([zeros(half), sin, zeros(HEAD_DIM - ROT)], axis=1)
    sin_b = jnp.concatenate([-sin, zeros(HEAD_DIM - half)], axis=1)
    return cos_t, sin_a, sin_b


def _perm_tables(tables, dil):
    return tuple(t.reshape(-1, dil, HEAD_DIM).transpose(1, 0, 2) for t in tables)


def _inproj_kernel(x_ref, g_ref, sc_ref, sh_ref, w_ref, cos_ref, sa_ref, sb_ref, o_ref, h_scr,
                   *maybe_slab_scr, n_q, n_rope, tn, dil):
    j = pl.program_id(1)
    tm, d = x_ref.shape
    n = tm // dil

    @pl.when(j == 0)
    def _():
        h = _norm_mod(x_ref[...], g_ref[...], sc_ref[...], sh_ref[...])
        if dil == 1:
            h_scr[...] = h.astype(BF16)
        else:
            (slab_scr,) = maybe_slab_scr
            for c in range(d // LANES):
                slab_scr[c] = h[:, c * LANES:(c + 1) * LANES]
            for r in range(dil):
                for c in range(d // LANES):
                    rows = slab_scr[c, pl.ds(r, n, stride=dil), :]
                    h_scr[r * n:(r + 1) * n, c * LANES:(c + 1) * LANES] = rows.astype(BF16)

    acc = _dot(h_scr[...], w_ref[...])
    is_rope = j < n_rope
    scale = jnp.where(j < n_q, Q_SCALE, 1.0).astype(F32)
    cos = jnp.where(is_rope, cos_ref[...].reshape(tm, HEAD_DIM) * scale, 1.0)
    sa = jnp.where(is_rope, sa_ref[...].reshape(tm, HEAD_DIM) * scale, 0.0)
    sb = jnp.where(is_rope, sb_ref[...].reshape(tm, HEAD_DIM) * scale, 0.0)
    for hh in range(tn // HEAD_DIM):
        a = acc[:, hh * HEAD_DIM:(hh + 1) * HEAD_DIM]
        r = a * cos + pltpu.roll(a, ROT // 2, 1) * sa + pltpu.roll(a, HEAD_DIM - ROT // 2, 1) * sb
        o_ref[:, :, hh * HEAD_DIM:(hh + 1) * HEAD_DIM] = r.reshape(dil, n, HEAD_DIM).astype(BF16)


def inproj(x, mod3, g, w, tables, seq, *, q_cols, rope_cols, tm, tn, dil=1, col_stride=1, col_off=0,
           n_cols=None):
    t, d = x.shape
    n_cols = n_cols or w.shape[1]
    tps = seq // tm
    n = tm // dil
    kern = functools.partial(_inproj_kernel, n_q=q_cols // tn, n_rope=rope_cols // tn, tn=tn, dil=dil)
    mod_spec = lambda k: pl.BlockSpec((None, 1, d), lambda i, j: (i // tps, 0, k))
    tab_spec = pl.BlockSpec((dil, n, HEAD_DIM), lambda i, j: (0, i % tps, 0))
    return pl.pallas_call(
        kern,
        out_shape=jax.ShapeDtypeStruct((t // seq, dil, seq // dil, n_cols), BF16),
        grid=(t // tm, n_cols // tn),
        in_specs=[
            pl.BlockSpec((tm, d), lambda i, j: (i, 0)),
            pl.BlockSpec((1, d), lambda i, j: (0, 0)),
            mod_spec(1), mod_spec(0),
            pl.BlockSpec((d, tn), lambda i, j: (0, j * col_stride + col_off)),
            tab_spec, tab_spec, tab_spec,
        ],
        out_specs=pl.BlockSpec((None, dil, n, tn), lambda i, j: (i // tps, 0, i % tps, j)),
        scratch_shapes=[pltpu.VMEM((tm, d), BF16)]
        + ([pltpu.VMEM((d // LANES, tm, LANES), F32)] if dil > 1 else []),
        compiler_params=_params(("parallel", "arbitrary")),
        name="inproj",
    )(x, g.reshape(1, d), mod3, mod3, w, *_perm_tables(tables, dil))


L_SAFE = 2.0 ** -64


def _attn_a_kernel(q_ref, k_ref, v_ref, lam_ref, subln_ref, o_ref, p_scr, knorm_scr,
                   *, tk, tk_pv, lam0, unroll):
    i = pl.program_id(2)
    seq = k_ref.shape[0]
    tq = q_ref.shape[0]
    dv = v_ref.shape[1]
    n_chunks = seq // tk

    @pl.when(i == 0)
    def _():
        def body(c, carry):
            off = pl.multiple_of(c * tk, tk)
            kf = k_ref[pl.ds(off, tk), :].astype(F32)
            sq = kf * kf
            n0 = jnp.max(jnp.sum(sq[:, :HEAD_DIM], axis=1, keepdims=True), axis=0, keepdims=True)
            n1 = jnp.max(jnp.sum(sq[:, HEAD_DIM:], axis=1, keepdims=True), axis=0, keepdims=True)
            return jnp.maximum(carry[0], n0), jnp.maximum(carry[1], n1)

        zero = jnp.zeros((1, 1), F32)
        n0, n1 = lax.fori_loop(0, n_chunks, body, (zero, zero))
        knorm_scr[0:1, :] = jnp.broadcast_to(jnp.sqrt(n0), (1, LANES))
        knorm_scr[1:2, :] = jnp.broadcast_to(jnp.sqrt(n1), (1, LANES))

    q = q_ref[...]
    qs = (q[:, :HEAD_DIM], q[:, HEAD_DIM:])
    lp = lam_ref[...].astype(F32)
    lam = (jnp.exp(jnp.sum(lp[0:1] * lp[1:2], axis=1, keepdims=True))
           - jnp.exp(jnp.sum(lp[2:3] * lp[3:4], axis=1, keepdims=True)) + lam0)

    def finish(o):
        ms = jnp.mean(o * o, axis=1, keepdims=True)
        y = (o * lax.rsqrt(ms + NORM_EPS)) * subln_ref[...]
        o_ref[...] = (y * (1.0 - lam0)).astype(BF16)

    ls = []
    for mp in range(2):
        qc = qs[mp]
        qf = qc.astype(F32)
        qn = jnp.sqrt(jnp.sum(qf * qf, axis=1, keepdims=True))
        shift = qn * (knorm_scr[mp:mp + 1, :] * 1.001) + 1.0
        cols = slice(mp * HEAD_DIM, (mp + 1) * HEAD_DIM)

        def chunk(c, lpart, mp=mp, qc=qc, shift=shift, cols=cols):
            off = pl.multiple_of(c * tk, tk)
            s = lax.dot_general(qc, k_ref[pl.ds(off, tk), cols], NT_DIMS, preferred_element_type=F32)
            for g in range(tk // LANES):
                p = jnp.exp2(s[:, g * LANES:(g + 1) * LANES] - shift)
                lpart = lpart + p
                p_scr[mp, :, pl.ds(pl.multiple_of(off + g * LANES, LANES), LANES)] = p
            return lpart

        lpart = lax.fori_loop(0, n_chunks, chunk, jnp.zeros((tq, LANES), F32), unroll=unroll)
        ls.append(jnp.sum(lpart, axis=1, keepdims=True))

    l_min = jnp.min(jnp.minimum(ls[0], ls[1]))

    @pl.when(l_min >= L_SAFE)
    def _():
        ratio = jnp.broadcast_to(lam * ls[0] / ls[1], (tq, LANES))
        acc = jnp.zeros((tq, dv), F32)
        for c in range(seq // tk_pv):
            groups = []
            for g in range(tk_pv // LANES):
                cs = slice(c * tk_pv + g * LANES, c * tk_pv + (g + 1) * LANES)
                groups.append((p_scr[0, :, cs] - ratio * p_scr[1, :, cs]).astype(BF16))
            a = jnp.concatenate(groups, axis=1)
            acc = acc + _dot(a, v_ref[c * tk_pv:(c + 1) * tk_pv, :])
        finish(acc / ls[0])

    @pl.when(l_min < L_SAFE)
    def _():
        def update(qc, kc, v, m, l, acc):
            s = lax.dot_general(qc, kc, NT_DIMS, preferred_element_type=F32)
            m_new = jnp.maximum(m, jnp.max(s, axis=1, keepdims=True))
            alpha = jnp.exp2(m - m_new)
            p = jnp.exp2(s - m_new)
            l = alpha * l + jnp.sum(p, axis=1, keepdims=True)
            acc = alpha * acc + _dot(p.astype(BF16), v)
            return m_new, l, acc

        def body(c, carry):
            m0, l0, a0, m1, l1, a1 = carry
            off = pl.multiple_of(c * tk, tk)
            k = k_ref[pl.ds(off, tk), :]
            v = v_ref[pl.ds(off, tk), :]
            m0, l0, a0 = update(qs[0], k[:, :HEAD_DIM], v, m0, l0, a0)
            m1, l1, a1 = update(qs[1], k[:, HEAD_DIM:], v, m1, l1, a1)
            return m0, l0, a0, m1, l1, a1

        m_init = jnp.full((tq, 1), NEG, F32)
        l_init = jnp.zeros((tq, 1), F32)
        a_init = jnp.zeros((tq, dv), F32)
        _, l0, a0, _, l1, a1 = lax.fori_loop(
            0, n_chunks, body, (m_init, l_init, a_init, m_init, l_init, a_init))
        finish(a0 / l0 - lam * (a1 / l1))


def attn_a(qkv, lam_p, subln, nb, seq, lam0, *, tq, tk, tk_pv, unroll):
    t = qkv.shape[0]
    dv = 2 * HEAD_DIM
    nq = seq // tq
    kern = functools.partial(_attn_a_kernel, tk=tk, tk_pv=tk_pv, lam0=lam0, unroll=unroll)
    return pl.pallas_call(
        kern,
        out_shape=jax.ShapeDtypeStruct((t, A_HEADS * dv), BF16),
        grid=(nb, A_HEADS, nq),
        in_specs=[
            pl.BlockSpec((tq, dv), lambda b, h, i: (b * nq + i, h)),
            pl.BlockSpec((seq, dv), lambda b, h, i: (b, A_HEADS + h)),
            pl.BlockSpec((seq, dv), lambda b, h, i: (b, 2 * A_HEADS + h)),
            pl.BlockSpec((4, HEAD_DIM), lambda b, h, i: (0, 0)),
            pl.BlockSpec((1, dv), lambda b, h, i: (0, 0)),
        ],
        out_specs=pl.BlockSpec((tq, dv), lambda b, h, i: (b * nq + i, h)),
        scratch_shapes=[pltpu.VMEM((2, tq, seq), F32), pltpu.VMEM((8, LANES), F32)],
        compiler_params=_params(("parallel", "parallel", "arbitrary")),
        name="attn_a",
    )(qkv, qkv, qkv, lam_p, subln.reshape(1, dv))


def _band_kernel(*refs, half, tl, seq_len, n_kv, group, has_sink, emit_lse):
    if has_sink:
        sink_ref, refs = refs[0], refs[1:]
    q_ref, kp_ref, kc_ref, kn_ref, vp_ref, vc_ref, vn_ref = refs[:7]
    o_ref = refs[7]
    lse_ref = refs[8] if emit_lse else None
    i = pl.program_id(1)
    win = tl + 2 * half
    qpos = i * tl + lax.broadcasted_iota(jnp.int32, (tl, win), 0)
    kpos = i * tl - half + lax.broadcasted_iota(jnp.int32, (tl, win), 1)
    dist = jnp.abs(kpos - qpos)
    dist = jnp.where(kpos < 0, win, dist)
    dist = jnp.where(kpos >= seq_len, win, dist)
    valid = dist <= half
    for kv in range(n_kv):
        cs = slice(kv * HEAD_DIM, (kv + 1) * HEAD_DIM)
        kw = jnp.concatenate([kp_ref[:, cs], kc_ref[:, cs], kn_ref[:, cs]], axis=0)
        vw = jnp.concatenate([vp_ref[:, cs], vc_ref[:, cs], vn_ref[:, cs]], axis=0)
        for g in range(group):
            hd = kv * group + g
            hs = slice(hd * HEAD_DIM, (hd + 1) * HEAD_DIM)
            s = lax.dot_general(q_ref[:, hs], kw, NT_DIMS, preferred_element_type=F32)
            s = jnp.where(valid, s, NEG)
            m = jnp.max(s, axis=1, keepdims=True)
            if has_sink:
                sk = sink_ref[hd] * LOG2E
                m = jnp.maximum(m, sk)
            e = jnp.exp2(s - m)
            den = jnp.sum(e, axis=1, keepdims=True)
            if has_sink:
                den = den + jnp.exp2(sk - m)
            o = _dot(e.astype(BF16), vw) / den
            o_ref[:, hs] = o.astype(o_ref.dtype)
            if emit_lse:
                lse_ref[:, hs] = jnp.broadcast_to(m + jnp.log2(den), (tl, HEAD_DIM))


def band_attn(qkv, *, q_col, k_col, v_col, n_kv, group, half, tl, sink=None, emit_lse=False):
    nb, seq_len, _ = qkv.shape
    wq = n_kv * group * HEAD_DIM
    wk = n_kv * HEAD_DIM
    nq = seq_len // tl
    hb = tl // half
    nh = seq_len // half
    kern = functools.partial(_band_kernel, half=half, tl=tl, seq_len=seq_len, n_kv=n_kv, group=group,
                             has_sink=sink is not None, emit_lse=emit_lse)

    def halo(col):
        prev = pl.BlockSpec((None, half, wk), lambda b, i: (b, jnp.maximum(i * hb - 1, 0), col))
        cur = pl.BlockSpec((None, tl, wk), lambda b, i: (b, i, col))
        nxt = pl.BlockSpec((None, half, wk), lambda b, i: (b, jnp.minimum((i + 1) * hb, nh - 1), col))
        return [prev, cur, nxt]

    in_specs = [pl.BlockSpec((None, tl, wq), lambda b, i: (b, i, q_col))] + halo(k_col) + halo(v_col)
    args = [qkv] * 7
    if sink is not None:
        in_specs = [pl.BlockSpec(memory_space=pltpu.SMEM)] + in_specs
        args = [sink] + args
    out_spec = pl.BlockSpec((None, tl, wq), lambda b, i: (b, i, 0))
    o_shape = jax.ShapeDtypeStruct((nb, seq_len, wq), BF16)
    if emit_lse:
        out_shape = (o_shape, jax.ShapeDtypeStruct((nb, seq_len, wq), F32))
        out_specs = (out_spec, out_spec)
    else:
        out_shape, out_specs = o_shape, out_spec
    return pl.pallas_call(
        kern,
        out_shape=out_shape,
        grid=(nb, nq),
        in_specs=in_specs,
        out_specs=out_specs,
        compiler_params=_params(("parallel", "arbitrary")),
        name="band_attn",
    )(*args)


def _outproj_kernel(a_ref, w_ref, x_ref, gate_ref, o_ref):
    o_ref[...] = x_ref[...] + gate_ref[...] * _dot(a_ref[...], w_ref[...])


def outproj(a, w, x, mod3, gate_chunk, seq, *, tm, tn):
    t, k = a.shape
    d = w.shape[1]
    tps = seq // tm
    nj = d // tn
    return pl.pallas_call(
        _outproj_kernel,
        out_shape=jax.ShapeDtypeStruct((t, d), F32),
        grid=(t // tm, nj),
        in_specs=[
            pl.BlockSpec((tm, k), lambda i, j: (i, 0)),
            pl.BlockSpec((k, tn), lambda i, j: (0, j)),
            pl.BlockSpec((tm, tn), lambda i, j: (i, j)),
            pl.BlockSpec((None, 1, tn), lambda i, j: (i // tps, 0, gate_chunk * nj + j)),
        ],
        out_specs=pl.BlockSpec((tm, tn), lambda i, j: (i, j)),
        compiler_params=_params(("parallel", "arbitrary")),
        name="outproj",
    )(a, w, x, mod3)


def _merge_outproj_kernel(*refs, dils):
    n_g = len(dils)
    o_refs, l_refs = refs[:n_g], refs[n_g:2 * n_g]
    w_ref, x_ref, gate_ref, o_ref, a_scr = refs[2 * n_g:2 * n_g + 5]
    un_scr = refs[2 * n_g + 5:]
    j = pl.program_id(1)

    @pl.when(j == 0)
    def _():
        os_, ls = [], []
        si = 0
        for og_ref, lg_ref, dil in zip(o_refs, l_refs, dils):
            if dil == 1:
                os_.append(og_ref[0].astype(F32))
                ls.append(lg_ref[0])
            else:
                so, sl = un_scr[si], un_scr[si + 1]
                si += 2
                n, k = og_ref.shape[1:]
                for r in range(dil):
                    for c in range(k // LANES):
                        cols = slice(c * LANES, (c + 1) * LANES)
                        so[c, pl.ds(r, n, stride=dil), :] = og_ref[r, :, cols].astype(F32)
                        sl[c, pl.ds(r, n, stride=dil), :] = lg_ref[r, :, cols]
                os_.append(jnp.concatenate([so[c] for c in range(k // LANES)], axis=1))
                ls.append(jnp.concatenate([sl[c] for c in range(k // LANES)], axis=1))
        m = functools.reduce(jnp.maximum, ls)
        es = [jnp.exp2(l - m) for l in ls]
        den = functools.reduce(lambda a, b: a + b, es)
        merged = functools.reduce(lambda a, b: a + b, [(e / den) * o for e, o in zip(es, os_)])
        a_scr[...] = merged.astype(BF16)

    o_ref[...] = x_ref[...] + gate_ref[...] * _dot(a_scr[...], w_ref[...])


def merge_outproj(os_, lses, dils, w, x, mod3, gate_chunk, seq, *, tm, tn):
    k = os_[0].shape[-1]
    t, d = x.shape
    tps = seq // tm
    nj = d // tn
    grp = lambda dil: pl.BlockSpec((None, dil, tm // dil, k), lambda i, j: (i // tps, 0, i % tps, 0))
    n_un = 2 * sum(dil != 1 for dil in dils)
    kern = functools.partial(_merge_outproj_kernel, dils=tuple(dils))
    return pl.pallas_call(
        kern,
        out_shape=jax.ShapeDtypeStruct((t, d), F32),
        grid=(t // tm, nj),
        in_specs=[grp(dil) for dil in dils] * 2 + [
            pl.BlockSpec((k, tn), lambda i, j: (0, j)),
            pl.BlockSpec((tm, tn), lambda i, j: (i, j)),
            pl.BlockSpec((None, 1, tn), lambda i, j: (i // tps, 0, gate_chunk * nj + j)),
        ],
        out_specs=pl.BlockSpec((tm, tn), lambda i, j: (i, j)),
        scratch_shapes=[pltpu.VMEM((tm, k), BF16)] + [pltpu.VMEM((k // LANES, tm, LANES), F32)] * n_un,
        compiler_params=_params(("parallel", "arbitrary")),
        name="merge_outproj",
    )(*os_, *lses, w, x, mod3)


def _silu_mul(g, u):
    return (g * jax.nn.sigmoid(g)) * u


def _ffn_up_kernel(x_ref, g_ref, sc_ref, sh_ref, wg_ref, wu_ref, o_ref, h_scr):
    j = pl.program_id(1)

    @pl.when(j == 0)
    def _():
        h_scr[...] = _norm_mod(x_ref[...], g_ref[...], sc_ref[...], sh_ref[...]).astype(BF16)

    h = h_scr[...]
    o_ref[...] = _silu_mul(_dot(h, wg_ref[...]), _dot(h, wu_ref[...])).astype(BF16)


def ffn_up(x, mod3, g, w_gu, seq, *, tm, tn):
    t, d = x.shape
    f = w_gu.shape[1] // 2
    tps = seq // tm
    nj = f // tn
    mod_spec = lambda k: pl.BlockSpec((None, 1, d), lambda i, j: (i // tps, 0, k))
    return pl.pallas_call(
        _ffn_up_kernel,
        out_shape=jax.ShapeDtypeStruct((t, f), BF16),
        grid=(t // tm, nj),
        in_specs=[
            pl.BlockSpec((tm, d), lambda i, j: (i, 0)),
            pl.BlockSpec((1, d), lambda i, j: (0, 0)),
            mod_spec(4), mod_spec(3),
            pl.BlockSpec((d, tn), lambda i, j: (0, j)),
            pl.BlockSpec((d, tn), lambda i, j: (0, nj + j)),
        ],
        out_specs=pl.BlockSpec((tm, tn), lambda i, j: (i, j)),
        scratch_shapes=[pltpu.VMEM((tm, d), BF16)],
        compiler_params=_params(("parallel", "arbitrary")),
        name="ffn_up",
    )(x, g.reshape(1, d), mod3, mod3, w_gu, w_gu)


def _top2_gates(logits):
    lane = lax.broadcasted_iota(jnp.int32, logits.shape, 1).astype(F32)
    logits = jnp.where(lane < N_EXPERTS, logits, NEG)
    v1 = jnp.max(logits, axis=1, keepdims=True)
    i1 = jnp.min(jnp.where(logits == v1, lane, float(LANES)), axis=1, keepdims=True)
    rest = jnp.where(lane == i1, NEG, logits)
    v2 = jnp.max(rest, axis=1, keepdims=True)
    i2 = jnp.min(jnp.where(rest == v2, lane, float(LANES)), axis=1, keepdims=True)
    e2 = jnp.exp(v2 - v1)
    w1 = 1.0 / (1.0 + e2)
    w2 = e2 / (1.0 + e2)
    return jnp.where(lane == i1, w1, 0.0) + jnp.where(lane == i2, w2, 0.0)


def _moe_up_kernel(x_ref, g_ref, sc_ref, sh_ref, r_ref, wg_ref, wu_ref, o_ref, h_scr, gate_scr,
                   *, tiles_per_expert):
    j = pl.program_id(1)

    @pl.when(j == 0)
    def _():
        h = _norm_mod(x_ref[...], g_ref[...], sc_ref[...], sh_ref[...])
        h_scr[...] = h.astype(BF16)
        gates = _top2_gates(_dot3(h, r_ref[...]))
        lane = lax.broadcasted_iota(jnp.int32, gates.shape, 1)
        for e in range(N_EXPERTS):
            col = jnp.sum(jnp.where(lane == e, gates, 0.0), axis=1, keepdims=True)
            gate_scr[e] = jnp.broadcast_to(col, gates.shape)

    h = h_scr[...]
    act = _silu_mul(_dot(h, wg_ref[...]), _dot(h, wu_ref[...]))
    gate = gate_scr[j // tiles_per_expert]
    tn = act.shape[1]
    for c in range(tn // LANES):
        o_ref[:, c * LANES:(c + 1) * LANES] = (act[:, c * LANES:(c + 1) * LANES] * gate).astype(BF16)


def moe_up(x, mod3, g, router, w_gu, seq, *, tm, tn):
    t, d = x.shape
    n_e, _, two_f = w_gu.shape
    f = two_f // 2
    tps = seq // tm
    tpe = f // tn
    mod_spec = lambda k: pl.BlockSpec((None, 1, d), lambda i, j: (i // tps, 0, k))
    router_p = jnp.pad(router, ((0, 0), (0, LANES - n_e)))
    kern = functools.partial(_moe_up_kernel, tiles_per_expert=tpe)
    return pl.pallas_call(
        kern,
        out_shape=jax.ShapeDtypeStruct((t, n_e * f), BF16),
        grid=(t // tm, n_e * tpe),
        in_specs=[
            pl.BlockSpec((tm, d), lambda i, j: (i, 0)),
            pl.BlockSpec((1, d), lambda i, j: (0, 0)),
            mod_spec(4), mod_spec(3),
            pl.BlockSpec((d, LANES), lambda i, j: (0, 0)),
            pl.BlockSpec((None, d, tn), lambda i, j: (j // tpe, 0, j % tpe)),
            pl.BlockSpec((None, d, tn), lambda i, j: (j // tpe, 0, tpe + j % tpe)),
        ],
        out_specs=pl.BlockSpec((tm, tn), lambda i, j: (i, j)),
        scratch_shapes=[pltpu.VMEM((tm, d), BF16), pltpu.VMEM((N_EXPERTS, tm, LANES), F32)],
        compiler_params=_params(("parallel", "arbitrary")),
        name="moe_up",
    )(x, g.reshape(1, d), mod3, mod3, router_p, w_gu, w_gu)


MOE_TM = 512
R_E1, R_E2, R_W1, R_W2, R_RANK1, R_RANK2 = range(6)


def _moe_route_kernel(x_ref, g_ref, sc_ref, sh_ref, r_ref, tri_ref, route_ref, cnt_ref, cnt_scr):
    i = pl.program_id(0)

    @pl.when(i == 0)
    def _():
        cnt_scr[...] = jnp.zeros_like(cnt_scr)

    h = _norm_mod(x_ref[...], g_ref[...], sc_ref[...], sh_ref[...])
    logits = _dot3(h, r_ref[...])
    lane = lax.broadcasted_iota(jnp.int32, logits.shape, 1).astype(F32)
    logits = jnp.where(lane < N_EXPERTS, logits, NEG)
    v1 = jnp.max(logits, axis=1, keepdims=True)
    i1 = jnp.min(jnp.where(logits == v1, lane, float(LANES)), axis=1, keepdims=True)
    rest = jnp.where(lane == i1, NEG, logits)
    v2 = jnp.max(rest, axis=1, keepdims=True)
    i2 = jnp.min(jnp.where(rest == v2, lane, float(LANES)), axis=1, keepdims=True)
    e2 = jnp.exp(v2 - v1)
    w1 = 1.0 / (1.0 + e2)
    w2 = e2 / (1.0 + e2)
    sel = jnp.where(lane == i1, 1.0, 0.0) + jnp.where(lane == i2, 1.0, 0.0)
    base = cnt_scr[0:1, :]
    rank = _dot(tri_ref[...], sel.astype(BF16)) + base
    r1 = jnp.sum(jnp.where(lane == i1, rank, 0.0), axis=1, keepdims=True)
    r2 = jnp.sum(jnp.where(lane == i2, rank, 0.0), axis=1, keepdims=True)
    slab = jnp.zeros_like(logits)
    for col, val in ((R_E1, i1), (R_E2, i2), (R_W1, w1), (R_W2, w2), (R_RANK1, r1), (R_RANK2, r2)):
        slab = jnp.where(lane == col, val, slab)
    route_ref[...] = slab
    total = base + jnp.sum(sel, axis=0, keepdims=True)
    cnt_scr[0:1, :] = total
    cnt_ref[...] = jnp.broadcast_to(total, cnt_ref.shape)


def moe_route(x, mod3, g, router, seq, *, tm):
    t, d = x.shape
    tps = seq // tm
    mod_spec = lambda k: pl.BlockSpec((None, 1, d), lambda i: (i // tps, 0, k))
    router_p = jnp.pad(router, ((0, 0), (0, LANES - router.shape[1])))
    tri = (lax.broadcasted_iota(jnp.int32, (tm, tm), 1)
           < lax.broadcasted_iota(jnp.int32, (tm, tm), 0)).astype(BF16)
    return pl.pallas_call(
        _moe_route_kernel,
        out_shape=(jax.ShapeDtypeStruct((t, LANES), F32), jax.ShapeDtypeStruct((8, LANES), F32)),
        grid=(t // tm,),
        in_specs=[
            pl.BlockSpec((tm, d), lambda i: (i, 0)),
            pl.BlockSpec((1, d), lambda i: (0, 0)),
            mod_spec(4), mod_spec(3),
            pl.BlockSpec((d, LANES), lambda i: (0, 0)),
            pl.BlockSpec((tm, tm), lambda i: (0, 0)),
        ],
        out_specs=(pl.BlockSpec((tm, LANES), lambda i: (i, 0)), pl.BlockSpec((8, LANES), lambda i: (0, 0))),
        scratch_shapes=[pltpu.VMEM((8, LANES), F32)],
        compiler_params=_params(("arbitrary",)),
        name="moe_route",
    )(x, g.reshape(1, d), mod3, mod3, router_p, tri)


def _moe_plan(route, counts, n_tok, tm):
    n_tiles_max = 2 * n_tok // MOE_TM + N_EXPERTS
    cnt = counts[0, :N_EXPERTS].astype(jnp.int32)
    tiles = (cnt + MOE_TM - 1) // MOE_TM
    tile_end = jnp.cumsum(tiles)
    tile_start = tile_end - tiles
    row_start = tile_start * MOE_TM
    n_tiles = tile_end[-1]
    e1 = route[:, R_E1].astype(jnp.int32)
    e2 = route[:, R_E2].astype(jnp.int32)
    pos1 = row_start[e1] + route[:, R_RANK1].astype(jnp.int32)
    pos2 = row_start[e2] + route[:, R_RANK2].astype(jnp.int32)
    shape = (n_tok // tm, 1, tm)
    tile_id = jnp.arange(n_tiles_max, dtype=jnp.int32)
    tile_expert = jnp.minimum(jnp.sum((tile_id[:, None] >= tile_end[None, :]).astype(jnp.int32), axis=1),
                              N_EXPERTS - 1)
    last_tile = jnp.where(tiles > 0, tile_end - 1, -1)
    tail_tile = n_tiles + jnp.arange(N_EXPERTS, dtype=jnp.int32)
    tail_tile = jnp.where(tail_tile < n_tiles_max, tail_tile, -1)
    zero_tiles = jnp.concatenate([last_tile, tail_tile])
    zero_rows = jnp.where(zero_tiles >= 0, zero_tiles * MOE_TM, -1).astype(jnp.int32)
    return pos1.reshape(shape), pos2.reshape(shape), tile_expert.astype(jnp.int32), zero_rows, n_tiles_max


def _moe_dispatch_kernel(zero_ref, pos1_ref, pos2_ref, x_ref, g_ref, sc_ref, sh_ref, xs_ref,
                         h_scr, zero_scr, sem, zsem):
    i = pl.program_id(0)
    tm = x_ref.shape[0]

    @pl.when(i == 0)
    def _():
        zero_scr[...] = jnp.zeros_like(zero_scr)
        for z in range(2 * N_EXPERTS):
            @pl.when(zero_ref[z] >= 0)
            def _():
                row = pl.multiple_of(zero_ref[z], MOE_TM)
                pltpu.make_async_copy(zero_scr, xs_ref.at[pl.ds(row, MOE_TM), :], zsem).start()
        for z in range(2 * N_EXPERTS):
            @pl.when(zero_ref[z] >= 0)
            def _():
                pltpu.make_async_copy(zero_scr, xs_ref.at[pl.ds(0, MOE_TM), :], zsem).wait()

    h_scr[...] = _norm_mod(x_ref[...], g_ref[...], sc_ref[...], sh_ref[...])

    def issue(r, carry):
        src = h_scr.at[pl.ds(r, 1), :]
        pltpu.make_async_copy(src, xs_ref.at[pl.ds(pos1_ref[0, r], 1), :], sem).start()
        pltpu.make_async_copy(src, xs_ref.at[pl.ds(pos2_ref[0, r], 1), :], sem).start()
        return carry

    lax.fori_loop(0, tm, issue, 0, unroll=8)
    for _ in range(2):
        pltpu.make_async_copy(h_scr, xs_ref.at[pl.ds(0, tm), :], sem).wait()


def moe_dispatch(x, mod3, g, pos1, pos2, zero_rows, n_rows, seq, *, tm):
    t, d = x.shape
    tps = seq // tm
    mod_spec = lambda k: pl.BlockSpec((None, 1, d), lambda i, *_: (i // tps, 0, k))
    pos_spec = pl.BlockSpec((None, 1, tm), lambda i, *_: (i, 0, 0), memory_space=pltpu.SMEM)
    return pl.pallas_call(
        _moe_dispatch_kernel,
        out_shape=jax.ShapeDtypeStruct((n_rows, d), F32),
        grid_spec=pltpu.PrefetchScalarGridSpec(
            num_scalar_prefetch=1,
            grid=(t // tm,),
            in_specs=[
                pos_spec, pos_spec,
                pl.BlockSpec((tm, d), lambda i, *_: (i, 0)),
                pl.BlockSpec((1, d), lambda i, *_: (0, 0)),
                mod_spec(4), mod_spec(3),
            ],
            out_specs=pl.BlockSpec(memory_space=pl.ANY),
            scratch_shapes=[pltpu.VMEM((tm, d), F32), pltpu.VMEM((MOE_TM, d), F32),
                            pltpu.SemaphoreType.DMA, pltpu.SemaphoreType.DMA],
        ),
        compiler_params=_params(("arbitrary",)),
        name="moe_dispatch",
    )(zero_rows, pos1, pos2, x, g.reshape(1, d), mod3, mod3)


def _moe_expert_kernel(expert_ref, xs_ref, wgu_ref, wd_ref, ys_ref):
    del expert_ref
    f = wd_ref.shape[0]
    gu = _dot(xs_ref[...].astype(BF16), wgu_ref[...])
    act = _silu_mul(gu[:, :f], gu[:, f:]).astype(BF16)
    ys_ref[...] = _dot(act, wd_ref[...])


def moe_expert(xs, w_gu, w_down, tile_expert):
    n_rows, d = xs.shape
    _, _, two_f = w_gu.shape
    return pl.pallas_call(
        _moe_expert_kernel,
        out_shape=jax.ShapeDtypeStruct((n_rows, d), F32),
        grid_spec=pltpu.PrefetchScalarGridSpec(
            num_scalar_prefetch=1,
            grid=(n_rows // MOE_TM,),
            in_specs=[
                pl.BlockSpec((MOE_TM, d), lambda s, ex: (s, 0)),
                pl.BlockSpec((None, d, two_f), lambda s, ex: (ex[s], 0, 0)),
                pl.BlockSpec((None, two_f // 2, d), lambda s, ex: (ex[s], 0, 0)),
            ],
            out_specs=pl.BlockSpec((MOE_TM, d), lambda s, ex: (s, 0)),
        ),
        compiler_params=_params(("arbitrary",)),
        name="moe_expert",
    )(tile_expert, xs, w_gu, w_down)


def _moe_combine_kernel(pos1_ref, pos2_ref, x_ref, route_ref, gate_ref, ys_ref, o_ref, y1_scr, y2_scr, sem):
    tm = x_ref.shape[0]

    def issue(r, carry):
        pltpu.make_async_copy(ys_ref.at[pl.ds(pos1_ref[0, r], 1), :], y1_scr.at[pl.ds(r, 1), :], sem).start()
        pltpu.make_async_copy(ys_ref.at[pl.ds(pos2_ref[0, r], 1), :], y2_scr.at[pl.ds(r, 1), :], sem).start()
        return carry

    lax.fori_loop(0, tm, issue, 0, unroll=8)
    for buf in (y1_scr, y2_scr):
        pltpu.make_async_copy(ys_ref.at[pl.ds(0, tm), :], buf, sem).wait()
    route = route_ref[...]
    w1 = route[:, R_W1:R_W1 + 1]
    w2 = route[:, R_W2:R_W2 + 1]
    o_ref[...] = x_ref[...] + gate_ref[...] * (w1 * y1_scr[...] + w2 * y2_scr[...])


def moe_combine(x, mod3, route, pos1, pos2, ys, seq, *, tm):
    t, d = x.shape
    tps = seq // tm
    pos_spec = pl.BlockSpec((None, 1, tm), lambda i: (i, 0, 0), memory_space=pltpu.SMEM)
    return pl.pallas_call(
        _moe_combine_kernel,
        out_shape=jax.ShapeDtypeStruct((t, d), F32),
        grid=(t // tm,),
        in_specs=[
            pos_spec, pos_spec,
            pl.BlockSpec((tm, d), lambda i: (i, 0)),
            pl.BlockSpec((tm, LANES), lambda i: (i, 0)),
            pl.BlockSpec((None, 1, d), lambda i: (i // tps, 0, 5)),
            pl.BlockSpec(memory_space=pl.ANY),
        ],
        out_specs=pl.BlockSpec((tm, d), lambda i: (i, 0)),
        scratch_shapes=[pltpu.VMEM((tm, d), F32), pltpu.VMEM((tm, d), F32), pltpu.SemaphoreType.DMA],
        compiler_params=_params(("arbitrary",)),
        name="moe_combine",
    )(pos1, pos2, x, route, mod3, ys)


def moe_ffn(x, mod3, g, router, w_gu, w_down, seq, *, tm):
    t = x.shape[0]
    route, counts = moe_route(x, mod3, g, router, seq, tm=tm)
    pos1, pos2, tile_expert, zero_rows, n_tiles_max = _moe_plan(route, counts, t, tm)
    xs = moe_dispatch(x, mod3, g, pos1, pos2, zero_rows, n_tiles_max * MOE_TM, seq, tm=tm)
    ys = moe_expert(xs, w_gu, w_down, tile_expert)
    return moe_combine(x, mod3, route, pos1, pos2, ys, seq, tm=tm)


def _final_norm_kernel(x_ref, g_ref, o_ref):
    x = x_ref[...]
    ms = jnp.mean(x * x, axis=-1, keepdims=True)
    o_ref[...] = (x * lax.rsqrt(ms + NORM_EPS)) * g_ref[...]


def final_norm(x, g, *, tm):
    t, d = x.shape
    return pl.pallas_call(
        _final_norm_kernel,
        out_shape=jax.ShapeDtypeStruct((t, d), F32),
        grid=(t // tm,),
        in_specs=[pl.BlockSpec((tm, d), lambda i: (i, 0)), pl.BlockSpec((1, d), lambda i: (0, 0))],
        out_specs=pl.BlockSpec((tm, d), lambda i: (i, 0)),
        compiler_params=_params(("parallel",)),
        name="final_norm",
    )(x, g.reshape(1, d))


def _lambda_init(layer):
    return 0.8 - 0.6 * math.exp(-0.3 * layer)


def _mixer_a(x, mod3, g, w_in, w_out, lam_p, subln, tables, nb, seq, layer, cfg):
    qkv = inproj(x, mod3, g, w_in, tables, seq, q_cols=D_MODEL, rope_cols=2 * D_MODEL,
                 tm=cfg["tm"], tn=cfg["tn_in"]).reshape(nb * seq, 3 * D_MODEL)
    o = attn_a(qkv, lam_p, subln, nb, seq, _lambda_init(layer), tq=cfg["tq_a"], tk=cfg["tk_a"],
               tk_pv=cfg["tk_pv_a"], unroll=cfg["unroll_a"])
    return outproj(o, w_out, x, mod3, 2, seq, tm=cfg["tm"], tn=cfg["tn_out_mix"])


def _mixer_b(x, mod3, g, w_in, w_out, sink, tables, nb, seq, cfg):
    nq = B_HEADS * HEAD_DIM
    qkv = inproj(x, mod3, g, w_in, tables, seq, q_cols=nq, rope_cols=nq + B_KV_HEADS * HEAD_DIM,
                 tm=cfg["tm"], tn=cfg["tn_in_b"]).reshape(nb, seq, -1)
    o = band_attn(qkv, q_col=0, k_col=4, v_col=5, n_kv=B_KV_HEADS, group=B_HEADS // B_KV_HEADS,
                  half=B_WINDOW, tl=cfg["tl_b"], sink=sink)
    return outproj(o.reshape(nb * seq, nq), w_out, x, mod3, 2, seq, tm=cfg["tm"], tn=cfg["tn_out_mix"])


def _mixer_c(x, mod3, g, w_in, w_out, tables, nb, seq, cfg):
    n_groups = len(C_CONFIGS)
    os_, lses, dils = [], [], []
    for gi, (window, dil) in enumerate(C_CONFIGS):
        qkv = inproj(x, mod3, g, w_in, tables, seq, q_cols=C_WIDTH, rope_cols=2 * C_WIDTH,
                     tm=cfg["tm"], tn=C_WIDTH, dil=dil, col_stride=n_groups, col_off=gi,
                     n_cols=3 * C_WIDTH)
        sub = seq // dil
        o, lse = band_attn(qkv.reshape(nb * dil, sub, 3 * C_WIDTH), q_col=0, k_col=1, v_col=2,
                           n_kv=C_HEADS, group=1, half=window // (2 * dil), tl=cfg["tl_c"],
                           emit_lse=True)
        os_.append(o.reshape(nb, dil, sub, C_WIDTH))
        lses.append(lse.reshape(nb, dil, sub, C_WIDTH))
        dils.append(dil)
    return merge_outproj(os_, lses, dils, w_out, x, mod3, 2, seq, tm=cfg["tm_merge"],
                         tn=cfg["tn_out_ffn"])


def _default_cfg(seq):
    return dict(tm=min(512, seq), tn_in=1024, tn_in_b=512, tn_out_mix=D_MODEL, tn_out_ffn=1024,
                tn_out_moe=512, tq_a=min(256, seq), tk_a=min(512, seq), tl_b=min(256, seq),
                tl_c=min(256, seq // C_CONFIGS[-1][1]), tm_merge=min(512, seq), tn_up=1024,
                tm_down=min(512, seq), unroll_a=True, tk_pv_a=min(2048, seq))


def trunk_all(x, c8, nb, seq, ada_w, ada_b, norm_mix, norm_ffn, a_w_in, a_w_out, a_lambda, a_subln,
              b_w_in, b_w_out, b_sink, c_w_in, c_w_out, f_w_gu, f_w_down, moe_router, moe_w_gu,
              moe_w_down, final_norm_g, cfg=None):
    cfg = cfg or _default_cfg(seq)
    bf = lambda w: w.astype(BF16)
    mod_all = adaln(c8, ada_w, ada_b)
    tables = _rope_tables(seq)
    for i in range(DEPTH):
        mod3 = mod_all[i].reshape(8, 1, 6 * D_MODEL)
        kind, jm = i % 3, i // 3
        if kind == 0:
            x = _mixer_a(x, mod3, norm_mix[i], bf(a_w_in[jm]), bf(a_w_out[jm]), a_lambda[jm],
                         a_subln[jm], tables, nb, seq, i, cfg)
        elif kind == 1:
            x = _mixer_b(x, mod3, norm_mix[i], bf(b_w_in[jm]), bf(b_w_out[jm]), b_sink[jm], tables,
                         nb, seq, cfg)
        else:
            x = _mixer_c(x, mod3, norm_mix[i], bf(c_w_in[jm]), bf(c_w_out[jm]), tables, nb, seq, cfg)
        jf = i // 2
        if i % 2 == 0:
            act = ffn_up(x, mod3, norm_ffn[i], bf(f_w_gu[jf]), seq, tm=cfg["tm"], tn=cfg["tn_up"])
            x = outproj(act, bf(f_w_down[jf]), x, mod3, 5, seq, tm=cfg["tm"], tn=cfg["tn_out_ffn"])
        else:
            x = moe_ffn(x, mod3, norm_ffn[i], moe_router[jf], bf(moe_w_gu[jf]), bf(moe_w_down[jf]), seq,
                        tm=cfg["tm"])
    return final_norm(x, final_norm_g, tm=cfg["tm_down"])


def kernel(x_prompt, x_sample, c_prompt, c_sample, ada_w, ada_b, norm_mix, norm_ffn, a_w_in, a_w_out,
           a_lambda, a_subln, b_w_in, b_w_out, b_sink, c_w_in, c_w_out, f_w_gu, f_w_down, moe_router,
           moe_w_gu, moe_w_down, final_norm):
    bp, seq, d = x_prompt.shape
    bs = x_sample.shape[0]
    assert x_sample.shape[1] == seq
    nb = bp + bs
    x = jnp.concatenate([x_prompt.reshape(bp * seq, d), x_sample.reshape(bs * seq, d)], axis=0)
    c8 = jnp.concatenate([c_prompt, c_sample, jnp.zeros((8 - nb, d), F32)], axis=0)
    y = trunk_all(x, c8, nb, seq, ada_w, ada_b, norm_mix, norm_ffn, a_w_in, a_w_out, a_lambda,
                  a_subln, b_w_in, b_w_out, b_sink, c_w_in, c_w_out, f_w_gu, f_w_down, moe_router,
                  moe_w_gu, moe_w_down, final_norm)
    return (y[:bp * seq].reshape(bp, seq, d), y[bp * seq:].reshape(bs, seq, d))
```

```python
import functools
import math

import jax
import jax.numpy as jnp
from jax import lax
from jax.experimental import pallas as pl
from jax.experimental.pallas import tpu as pltpu

F32 = jnp.float32
BF16 = jnp.bfloat16

D_MODEL = 2048
DEPTH = 4
HEAD_DIM = 128
ROPE_THETA = 500000.0
ROT = HEAD_DIM // 4
NORM_EPS = 1e-6
NEG = -1e30
A_HEADS = 8
B_HEADS = 16
B_KV_HEADS = 4
B_WINDOW = 128
C_CONFIGS = ((128, 1), (512, 4), (2048, 16))
C_HEADS = 8
C_WIDTH = C_HEADS * HEAD_DIM
FFN_DIM = 2 * D_MODEL
MOE_DIM = D_MODEL // 2
N_EXPERTS = 8

LOG2E = math.log2(math.e)
Q_SCALE = HEAD_DIM ** -0.5 * LOG2E

LANES = 128
VMEM_LIMIT = 56 * 1024 * 1024

NT_DIMS = (((1,), (1,)), ((), ()))


def _params(sem, vmem=VMEM_LIMIT):
    return pltpu.CompilerParams(dimension_semantics=sem, vmem_limit_bytes=vmem)


def _dot(a, b):
    return jnp.dot(a, b, preferred_element_type=F32)


def _split_bf16(x):
    hi = x.astype(BF16)
    lo = (x - hi.astype(F32)).astype(BF16)
    return hi, lo


def _dot3(a, b):
    a_hi, a_lo = _split_bf16(a)
    b_hi, b_lo = _split_bf16(b)
    return _dot(a_hi, b_hi) + (_dot(a_hi, b_lo) + _dot(a_lo, b_hi))


def _row_sources(x):
    return tuple(x) if isinstance(x, (tuple, list)) else (x,)


def _row_specs(xs, tm, width, col_of):
    if len(xs) == 1:
        return [pl.BlockSpec((tm, width), lambda i, *a: (i, col_of(*a)))]
    n_a = xs[0].shape[0] // tm
    return [pl.BlockSpec((tm, width), lambda i, *a: (jnp.minimum(i, n_a - 1), col_of(*a))),
            pl.BlockSpec((tm, width), lambda i, *a: (jnp.maximum(i - n_a, 0), col_of(*a)))]


def _read_rows(x_refs, n_a):
    if len(x_refs) == 1:
        return x_refs[0][...]
    return jnp.where(pl.program_id(0) < n_a, x_refs[0][...], x_refs[1][...])


def _norm_mod(x, g, sc, sh):
    ms = jnp.mean(x * x, axis=-1, keepdims=True)
    y = x * lax.rsqrt(ms + NORM_EPS)
    return (y * g) * (1.0 + sc) + sh


def _adaln_kernel(c_ref, w_ref, b_ref, o_ref):
    c = c_ref[...]
    cs = c * jax.nn.sigmoid(c)
    o_ref[...] = _dot3(cs, w_ref[...]) + b_ref[...]


def adaln(c8, ada_w, ada_b, tn=512):
    depth, d, n = ada_w.shape
    return pl.pallas_call(
        _adaln_kernel,
        out_shape=jax.ShapeDtypeStruct((depth, 8, n), F32),
        grid=(depth, n // tn),
        in_specs=[
            pl.BlockSpec((8, d), lambda l, j: (0, 0)),
            pl.BlockSpec((None, d, tn), lambda l, j: (l, 0, j)),
            pl.BlockSpec((None, 1, tn), lambda l, j: (l, 0, j)),
        ],
        out_specs=pl.BlockSpec((None, 8, tn), lambda l, j: (l, 0, j)),
        compiler_params=_params(("arbitrary", "arbitrary")),
        name="adaln",
    )(c8, ada_w, ada_b.reshape(depth, 1, n))


def _rope_tables(seq):
    pos = jnp.arange(seq, dtype=F32)
    half = ROT // 2
    inv = ROPE_THETA ** (-jnp.arange(half, dtype=F32) * (2.0 / ROT))
    ang = pos[:, None] * inv[None, :]
    cos, sin = jnp.cos(ang), jnp.sin(ang)
    zeros = lambda n: jnp.zeros((seq, n), F32)
    cos_t = jnp.concatenate([cos, cos, jnp.ones((seq, HEAD_DIM - ROT), F32)], axis=1)
    sin_a = jnp.concatenate([zeros(half), sin, zeros(HEAD_DIM - ROT)], axis=1)
    sin_b = jnp.concatenate([-sin, zeros(HEAD_DIM - half)], axis=1)
    return cos_t, sin_a, sin_b


def _perm_tables(tables, dil):
    return tuple(t.reshape(-1, dil, HEAD_DIM).transpose(1, 0, 2) for t in tables)


def _inproj_kernel(*refs, n_x, n_a, n_q, n_rope, tn, dil):
    x_refs = refs[:n_x]
    g_ref, sc_ref, sh_ref, w_ref, cos_ref, sa_ref, sb_ref, o_ref, h_scr = refs[n_x:n_x + 9]
    maybe_slab_scr = refs[n_x + 9:]
    j = pl.program_id(1)
    tm, d = x_refs[0].shape
    n = tm // dil

    @pl.when(j == 0)
    def _():
        h = _norm_mod(_read_rows(x_refs, n_a), g_ref[...], sc_ref[...], sh_ref[...])
        if dil == 1:
            h_scr[...] = h.astype(BF16)
        else:
            (slab_scr,) = maybe_slab_scr
            for c in range(d // LANES):
                slab_scr[c] = h[:, c * LANES:(c + 1) * LANES]
            for r in range(dil):
                for c in range(d // LANES):
                    rows = slab_scr[c, pl.ds(r, n, stride=dil), :]
                    h_scr[r * n:(r + 1) * n, c * LANES:(c + 1) * LANES] = rows.astype(BF16)

    @pl.when(j < n_rope)
    def _():
        scale = jnp.where(j < n_q, Q_SCALE, 1.0).astype(F32)
        cos = cos_ref[...].reshape(tm, HEAD_DIM) * scale
        sa = sa_ref[...].reshape(tm, HEAD_DIM) * scale
        sb = sb_ref[...].reshape(tm, HEAD_DIM) * scale
        acc = _dot(h_scr[...], w_ref[...])
        for hh in range(tn // HEAD_DIM):
            a = acc[:, hh * HEAD_DIM:(hh + 1) * HEAD_DIM]
            r = a * cos + pltpu.roll(a, ROT // 2, 1) * sa + pltpu.roll(a, HEAD_DIM - ROT // 2, 1) * sb
            o_ref[:, :, hh * HEAD_DIM:(hh + 1) * HEAD_DIM] = r.reshape(dil, n, HEAD_DIM).astype(BF16)

    @pl.when(j >= n_rope)
    def _():
        o_ref[...] = _dot(h_scr[...], w_ref[...]).reshape(dil, n, tn).astype(BF16)


def inproj(x, mod3, g, w, tables, seq, *, q_cols, rope_cols, tm, tn, dil=1, col_stride=1, col_off=0,
           n_cols=None):
    xs = _row_sources(x)
    t = sum(a.shape[0] for a in xs)
    d = xs[0].shape[1]
    n_cols = n_cols or w.shape[1]
    tps = seq // tm
    n = tm // dil
    kern = functools.partial(_inproj_kernel, n_x=len(xs), n_a=xs[0].shape[0] // tm, n_q=q_cols // tn,
                             n_rope=rope_cols // tn, tn=tn, dil=dil)
    mod_spec = lambda k: pl.BlockSpec((None, 1, d), lambda i, j: (i // tps, 0, k))
    tab_spec = pl.BlockSpec((dil, n, HEAD_DIM), lambda i, j: (0, i % tps, 0))
    return pl.pallas_call(
        kern,
        out_shape=jax.ShapeDtypeStruct((t // seq, dil, seq // dil, n_cols), BF16),
        grid=(t // tm, n_cols // tn),
        in_specs=_row_specs(xs, tm, d, lambda j: 0) + [
            pl.BlockSpec((1, d), lambda i, j: (0, 0)),
            mod_spec(1), mod_spec(0),
            pl.BlockSpec((d, tn), lambda i, j: (0, j * col_stride + col_off)),
            tab_spec, tab_spec, tab_spec,
        ],
        out_specs=pl.BlockSpec((None, dil, n, tn), lambda i, j: (i // tps, 0, i % tps, j)),
        scratch_shapes=[pltpu.VMEM((tm, d), BF16)]
        + ([pltpu.VMEM((d // LANES, tm, LANES), F32)] if dil > 1 else []),
        compiler_params=_params(("parallel", "arbitrary")),
        name="inproj",
    )(*xs, g.reshape(1, d), mod3, mod3, w, *_perm_tables(tables, dil))


L_SAFE = 2.0 ** -64


def _attn_a_kernel(q_ref, k_ref, v_ref, lam_ref, subln_ref, o_ref, p_scr, knorm_scr,
                   *, tk, tk_pv, lam0, unroll):
    i = pl.program_id(2)
    seq = k_ref.shape[0]
    tq = q_ref.shape[0]
    dv = v_ref.shape[1]
    n_chunks = seq // tk

    @pl.when(i == 0)
    def _():
        def body(c, carry):
            off = pl.multiple_of(c * tk, tk)
            kf = k_ref[pl.ds(off, tk), :].astype(F32)
            sq = kf * kf
            n0 = jnp.max(jnp.sum(sq[:, :HEAD_DIM], axis=1, keepdims=True), axis=0, keepdims=True)
            n1 = jnp.max(jnp.sum(sq[:, HEAD_DIM:], axis=1, keepdims=True), axis=0, keepdims=True)
            return jnp.maximum(carry[0], n0), jnp.maximum(carry[1], n1)

        zero = jnp.zeros((1, 1), F32)
        n0, n1 = lax.fori_loop(0, n_chunks, body, (zero, zero))
        knorm_scr[0:1, :] = jnp.broadcast_to(jnp.sqrt(n0), (1, LANES))
        knorm_scr[1:2, :] = jnp.broadcast_to(jnp.sqrt(n1), (1, LANES))

    q = q_ref[...]
    qs = (q[:, :HEAD_DIM], q[:, HEAD_DIM:])
    lp = lam_ref[...].astype(F32)
    lam = (jnp.exp(jnp.sum(lp[0:1] * lp[1:2], axis=1, keepdims=True))
           - jnp.exp(jnp.sum(lp[2:3] * lp[3:4], axis=1, keepdims=True)) + lam0)

    def finish(o):
        ms = jnp.mean(o * o, axis=1, keepdims=True)
        y = (o * lax.rsqrt(ms + NORM_EPS)) * subln_ref[...]
        o_ref[...] = (y * (1.0 - lam0)).astype(BF16)

    ls = []
    for mp in range(2):
        qc = qs[mp]
        qf = qc.astype(F32)
        qn = jnp.sqrt(jnp.sum(qf * qf, axis=1, keepdims=True))
        shift = qn * (knorm_scr[mp:mp + 1, :] * 1.001) + 1.0
        cols = slice(mp * HEAD_DIM, (mp + 1) * HEAD_DIM)

        def chunk(c, lpart, mp=mp, qc=qc, shift=shift, cols=cols):
            off = pl.multiple_of(c * tk, tk)
            s = lax.dot_general(qc, k_ref[pl.ds(off, tk), cols], NT_DIMS, preferred_element_type=F32)
            for g in range(tk // LANES):
                p = jnp.exp2(s[:, g * LANES:(g + 1) * LANES] - shift)
                lpart = lpart + p
                p_scr[mp, :, pl.ds(pl.multiple_of(off + g * LANES, LANES), LANES)] = p
            return lpart

        lpart = lax.fori_loop(0, n_chunks, chunk, jnp.zeros((tq, LANES), F32), unroll=unroll)
        ls.append(jnp.sum(lpart, axis=1, keepdims=True))

    l_min = jnp.min(jnp.minimum(ls[0], ls[1]))

    @pl.when(l_min >= L_SAFE)
    def _():
        ratio = jnp.broadcast_to(lam * ls[0] / ls[1], (tq, LANES))
        acc = jnp.zeros((tq, dv), F32)
        for c in range(seq // tk_pv):
            groups = []
            for g in range(tk_pv // LANES):
                cs = slice(c * tk_pv + g * LANES, c * tk_pv + (g + 1) * LANES)
                groups.append((p_scr[0, :, cs] - ratio * p_scr[1, :, cs]).astype(BF16))
            a = jnp.concatenate(groups, axis=1)
            acc = acc + _dot(a, v_ref[c * tk_pv:(c + 1) * tk_pv, :])
        finish(acc / ls[0])

    @pl.when(l_min < L_SAFE)
    def _():
        def update(qc, kc, v, m, l, acc):
            s = lax.dot_general(qc, kc, NT_DIMS, preferred_element_type=F32)
            m_new = jnp.maximum(m, jnp.max(s, axis=1, keepdims=True))
            alpha = jnp.exp2(m - m_new)
            p = jnp.exp2(s - m_new)
            l = alpha * l + jnp.sum(p, axis=1, keepdims=True)
            acc = alpha * acc + _dot(p.astype(BF16), v)
            return m_new, l, acc

        def body(c, carry):
            m0, l0, a0, m1, l1, a1 = carry
            off = pl.multiple_of(c * tk, tk)
            k = k_ref[pl.ds(off, tk), :]
            v = v_ref[pl.ds(off, tk), :]
            m0, l0, a0 = update(qs[0], k[:, :HEAD_DIM], v, m0, l0, a0)
            m1, l1, a1 = update(qs[1], k[:, HEAD_DIM:], v, m1, l1, a1)
            return m0, l0, a0, m1, l1, a1

        m_init = jnp.full((tq, 1), NEG, F32)
        l_init = jnp.zeros((tq, 1), F32)
        a_init = jnp.zeros((tq, dv), F32)
        _, l0, a0, _, l1, a1 = lax.fori_loop(
            0, n_chunks, body, (m_init, l_init, a_init, m_init, l_init, a_init))
        finish(a0 / l0 - lam * (a1 / l1))


def attn_a(qkv, lam_p, subln, nb, seq, lam0, *, tq, tk, tk_pv, unroll):
    t = qkv.shape[0]
    dv = 2 * HEAD_DIM
    nq = seq // tq
    kern = functools.partial(_attn_a_kernel, tk=tk, tk_pv=tk_pv, lam0=lam0, unroll=unroll)
    return pl.pallas_call(
        kern,
        out_shape=jax.ShapeDtypeStruct((t, A_HEADS * dv), BF16),
        grid=(nb, A_HEADS, nq),
        in_specs=[
            pl.BlockSpec((tq, dv), lambda b, h, i: (b * nq + i, h)),
            pl.BlockSpec((seq, dv), lambda b, h, i: (b, A_HEADS + h), pipeline_mode=pl.Buffered(1)),
            pl.BlockSpec((seq, dv), lambda b, h, i: (b, 2 * A_HEADS + h), pipeline_mode=pl.Buffered(1)),
            pl.BlockSpec((4, HEAD_DIM), lambda b, h, i: (0, 0)),
            pl.BlockSpec((1, dv), lambda b, h, i: (0, 0)),
        ],
        out_specs=pl.BlockSpec((tq, dv), lambda b, h, i: (b * nq + i, h)),
        scratch_shapes=[pltpu.VMEM((2, tq, seq), F32), pltpu.VMEM((8, LANES), F32)],
        compiler_params=_params(("parallel", "parallel", "arbitrary")),
        name="attn_a",
    )(qkv, qkv, qkv, lam_p, subln.reshape(1, dv))


def _band_kernel(*refs, half, tl, seq_len, n_kv, group, has_sink, emit_lse):
    if has_sink:
        sink_ref, refs = refs[0], refs[1:]
    q_ref, kp_ref, kc_ref, kn_ref, vp_ref, vc_ref, vn_ref = refs[:7]
    o_ref = refs[7]
    lse_ref = refs[8] if emit_lse else None
    i = pl.program_id(1)
    win = tl + 2 * half
    qpos = i * tl + lax.broadcasted_iota(jnp.int32, (tl, win), 0)
    kpos = i * tl - half + lax.broadcasted_iota(jnp.int32, (tl, win), 1)
    dist = jnp.abs(kpos - qpos)
    dist = jnp.where(kpos < 0, win, dist)
    dist = jnp.where(kpos >= seq_len, win, dist)
    valid = dist <= half
    for kv in range(n_kv):
        cs = slice(kv * HEAD_DIM, (kv + 1) * HEAD_DIM)
        kw = jnp.concatenate([kp_ref[:, cs], kc_ref[:, cs], kn_ref[:, cs]], axis=0)
        vw = jnp.concatenate([vp_ref[:, cs], vc_ref[:, cs], vn_ref[:, cs]], axis=0)
        for g in range(group):
            hd = kv * group + g
            hs = slice(hd * HEAD_DIM, (hd + 1) * HEAD_DIM)
            s = lax.dot_general(q_ref[:, hs], kw, NT_DIMS, preferred_element_type=F32)
            s = jnp.where(valid, s, NEG)
            m = jnp.max(s, axis=1, keepdims=True)
            if has_sink:
                sk = sink_ref[hd] * LOG2E
                m = jnp.maximum(m, sk)
            e = jnp.exp2(s - m)
            den = jnp.sum(e, axis=1, keepdims=True)
            if has_sink:
                den = den + jnp.exp2(sk - m)
            o = _dot(e.astype(BF16), vw) / den
            o_ref[:, hs] = o.astype(o_ref.dtype)
            if emit_lse:
                lse_ref[:, hs] = jnp.broadcast_to(m + jnp.log2(den), (tl, HEAD_DIM))


def band_attn(qkv, *, q_col, k_col, v_col, n_kv, group, half, tl, sink=None, emit_lse=False):
    nb, seq_len, _ = qkv.shape
    wq = n_kv * group * HEAD_DIM
    wk = n_kv * HEAD_DIM
    nq = seq_len // tl
    hb = tl // half
    nh = seq_len // half
    kern = functools.partial(_band_kernel, half=half, tl=tl, seq_len=seq_len, n_kv=n_kv, group=group,
                             has_sink=sink is not None, emit_lse=emit_lse)

    def halo(col):
        prev = pl.BlockSpec((None, half, wk), lambda b, i: (b, jnp.maximum(i * hb - 1, 0), col))
        cur = pl.BlockSpec((None, tl, wk), lambda b, i: (b, i, col))
        nxt = pl.BlockSpec((None, half, wk), lambda b, i: (b, jnp.minimum((i + 1) * hb, nh - 1), col))
        return [prev, cur, nxt]

    in_specs = [pl.BlockSpec((None, tl, wq), lambda b, i: (b, i, q_col))] + halo(k_col) + halo(v_col)
    args = [qkv] * 7
    if sink is not None:
        in_specs = [pl.BlockSpec(memory_space=pltpu.SMEM)] + in_specs
        args = [sink] + args
    out_spec = pl.BlockSpec((None, tl, wq), lambda b, i: (b, i, 0))
    o_shape = jax.ShapeDtypeStruct((nb, seq_len, wq), BF16)
    if emit_lse:
        out_shape = (o_shape, jax.ShapeDtypeStruct((nb, seq_len, wq), F32))
        out_specs = (out_spec, out_spec)
    else:
        out_shape, out_specs = o_shape, out_spec
    return pl.pallas_call(
        kern,
        out_shape=out_shape,
        grid=(nb, nq),
        in_specs=in_specs,
        out_specs=out_specs,
        compiler_params=_params(("parallel", "arbitrary")),
        name="band_attn",
    )(*args)


def _outproj_kernel(a_ref, w_ref, gate_ref, *refs, n_a):
    x_refs, o_ref = refs[:-1], refs[-1]
    o_ref[...] = _read_rows(x_refs, n_a) + gate_ref[...] * _dot(a_ref[...], w_ref[...])


def outproj(a, w, x, mod3, gate_chunk, seq, *, tm, tn):
    t, k = a.shape
    d = w.shape[1]
    tps = seq // tm
    nj = d // tn
    xs = _row_sources(x)
    return pl.pallas_call(
        functools.partial(_outproj_kernel, n_a=xs[0].shape[0] // tm),
        out_shape=jax.ShapeDtypeStruct((t, d), F32),
        grid=(t // tm, nj),
        in_specs=[
            pl.BlockSpec((tm, k), lambda i, j: (i, 0)),
            pl.BlockSpec((k, tn), lambda i, j: (0, j)),
            pl.BlockSpec((None, 1, tn), lambda i, j: (i // tps, 0, gate_chunk * nj + j)),
        ] + _row_specs(xs, tm, tn, lambda j: j),
        out_specs=pl.BlockSpec((tm, tn), lambda i, j: (i, j)),
        compiler_params=_params(("parallel", "arbitrary")),
        name="outproj",
    )(a, w, mod3, *xs)


def _merge_outproj_kernel(*refs, dils):
    n_g = len(dils)
    o_refs, l_refs = refs[:n_g], refs[n_g:2 * n_g]
    w_ref, x_ref, gate_ref, o_ref, a_scr = refs[2 * n_g:2 * n_g + 5]
    un_scr = refs[2 * n_g + 5:]
    j = pl.program_id(1)

    @pl.when(j == 0)
    def _():
        os_, ls = [], []
        si = 0
        for og_ref, lg_ref, dil in zip(o_refs, l_refs, dils):
            if dil == 1:
                os_.append(og_ref[0].astype(F32))
                ls.append(lg_ref[0])
            else:
                so, sl = un_scr[si], un_scr[si + 1]
                si += 2
                n, k = og_ref.shape[1:]
                for r in range(dil):
                    for c in range(k // LANES):
                        cols = slice(c * LANES, (c + 1) * LANES)
                        so[c, pl.ds(r, n, stride=dil), :] = og_ref[r, :, cols].astype(F32)
                        sl[c, pl.ds(r, n, stride=dil), :] = lg_ref[r, :, cols]
                os_.append(jnp.concatenate([so[c] for c in range(k // LANES)], axis=1))
                ls.append(jnp.concatenate([sl[c] for c in range(k // LANES)], axis=1))
        m = functools.reduce(jnp.maximum, ls)
        es = [jnp.exp2(l - m) for l in ls]
        den = functools.reduce(lambda a, b: a + b, es)
        merged = functools.reduce(lambda a, b: a + b, [(e / den) * o for e, o in zip(es, os_)])
        a_scr[...] = merged.astype(BF16)

    o_ref[...] = x_ref[...] + gate_ref[...] * _dot(a_scr[...], w_ref[...])


def merge_outproj(os_, lses, dils, w, x, mod3, gate_chunk, seq, *, tm, tn):
    k = os_[0].shape[-1]
    t, d = x.shape
    tps = seq // tm
    nj = d // tn
    grp = lambda dil: pl.BlockSpec((None, dil, tm // dil, k), lambda i, j: (i // tps, 0, i % tps, 0))
    n_un = 2 * sum(dil != 1 for dil in dils)
    kern = functools.partial(_merge_outproj_kernel, dils=tuple(dils))
    return pl.pallas_call(
        kern,
        out_shape=jax.ShapeDtypeStruct((t, d), F32),
        grid=(t // tm, nj),
        in_specs=[grp(dil) for dil in dils] * 2 + [
            pl.BlockSpec((k, tn), lambda i, j: (0, j)),
            pl.BlockSpec((tm, tn), lambda i, j: (i, j)),
            pl.BlockSpec((None, 1, tn), lambda i, j: (i // tps, 0, gate_chunk * nj + j)),
        ],
        out_specs=pl.BlockSpec((tm, tn), lambda i, j: (i, j)),
        scratch_shapes=[pltpu.VMEM((tm, k), BF16)] + [pltpu.VMEM((k // LANES, tm, LANES), F32)] * n_un,
        compiler_params=_params(("parallel", "arbitrary")),
        name="merge_outproj",
    )(*os_, *lses, w, x, mod3)


def _silu_mul(g, u):
    return (g * jax.nn.sigmoid(g)) * u


def _ffn_up_kernel(x_ref, g_ref, sc_ref, sh_ref, wg_ref, wu_ref, o_ref, h_scr):
    j = pl.program_id(1)

    @pl.when(j == 0)
    def _():
        h_scr[...] = _norm_mod(x_ref[...], g_ref[...], sc_ref[...], sh_ref[...]).astype(BF16)

    h = h_scr[...]
    o_ref[...] = _silu_mul(_dot(h, wg_ref[...]), _dot(h, wu_ref[...])).astype(BF16)


def ffn_up(x, mod3, g, w_gu, seq, *, tm, tn):
    t, d = x.shape
    f = w_gu.shape[1] // 2
    tps = seq // tm
    nj = f // tn
    mod_spec = lambda k: pl.BlockSpec((None, 1, d), lambda i, j: (i // tps, 0, k))
    return pl.pallas_call(
        _ffn_up_kernel,
        out_shape=jax.ShapeDtypeStruct((t, f), BF16),
        grid=(t // tm, nj),
        in_specs=[
            pl.BlockSpec((tm, d), lambda i, j: (i, 0)),
            pl.BlockSpec((1, d), lambda i, j: (0, 0)),
            mod_spec(4), mod_spec(3),
            pl.BlockSpec((d, tn), lambda i, j: (0, j)),
            pl.BlockSpec((d, tn), lambda i, j: (0, nj + j)),
        ],
        out_specs=pl.BlockSpec((tm, tn), lambda i, j: (i, j)),
        scratch_shapes=[pltpu.VMEM((tm, d), BF16)],
        compiler_params=_params(("parallel", "arbitrary")),
        name="ffn_up",
    )(x, g.reshape(1, d), mod3, mod3, w_gu, w_gu)


MOE_TM = 512
R_E1, R_E2, R_W1, R_W2, R_RANK1, R_RANK2 = range(6)


def _moe_route_kernel(x_ref, g_ref, sc_ref, sh_ref, r_ref, tri_ref, route_ref, cnt_ref, cnt_scr):
    i = pl.program_id(0)

    @pl.when(i == 0)
    def _():
        cnt_scr[...] = jnp.zeros_like(cnt_scr)

    h = _norm_mod(x_ref[...], g_ref[...], sc_ref[...], sh_ref[...])
    logits = _dot3(h, r_ref[...])
    lane = lax.broadcasted_iota(jnp.int32, logits.shape, 1).astype(F32)
    logits = jnp.where(lane < N_EXPERTS, logits, NEG)
    v1 = jnp.max(logits, axis=1, keepdims=True)
    i1 = jnp.min(jnp.where(logits == v1, lane, float(LANES)), axis=1, keepdims=True)
    rest = jnp.where(lane == i1, NEG, logits)
    v2 = jnp.max(rest, axis=1, keepdims=True)
    i2 = jnp.min(jnp.where(rest == v2, lane, float(LANES)), axis=1, keepdims=True)
    e2 = jnp.exp(v2 - v1)
    w1 = 1.0 / (1.0 + e2)
    w2 = e2 / (1.0 + e2)
    sel = jnp.where(lane == i1, 1.0, 0.0) + jnp.where(lane == i2, 1.0, 0.0)
    base = cnt_scr[0:1, :]
    rank = _dot(tri_ref[...], sel.astype(BF16)) + base
    r1 = jnp.sum(jnp.where(lane == i1, rank, 0.0), axis=1, keepdims=True)
    r2 = jnp.sum(jnp.where(lane == i2, rank, 0.0), axis=1, keepdims=True)
    slab = jnp.zeros_like(logits)
    for col, val in ((R_E1, i1), (R_E2, i2), (R_W1, w1), (R_W2, w2), (R_RANK1, r1), (R_RANK2, r2)):
        slab = jnp.where(lane == col, val, slab)
    route_ref[...] = slab
    total = base + jnp.sum(sel, axis=0, keepdims=True)
    cnt_scr[0:1, :] = total
    cnt_ref[...] = jnp.broadcast_to(total, cnt_ref.shape)


def moe_route(x, mod3, g, router, seq, *, tm):
    t, d = x.shape
    tps = seq // tm
    mod_spec = lambda k: pl.BlockSpec((None, 1, d), lambda i: (i // tps, 0, k))
    router_p = jnp.pad(router, ((0, 0), (0, LANES - router.shape[1])))
    tri = (lax.broadcasted_iota(jnp.int32, (tm, tm), 1)
           < lax.broadcasted_iota(jnp.int32, (tm, tm), 0)).astype(BF16)
    return pl.pallas_call(
        _moe_route_kernel,
        out_shape=(jax.ShapeDtypeStruct((t, LANES), F32), jax.ShapeDtypeStruct((8, LANES), F32)),
        grid=(t // tm,),
        in_specs=[
            pl.BlockSpec((tm, d), lambda i: (i, 0)),
            pl.BlockSpec((1, d), lambda i: (0, 0)),
            mod_spec(4), mod_spec(3),
            pl.BlockSpec((d, LANES), lambda i: (0, 0)),
            pl.BlockSpec((tm, tm), lambda i: (0, 0)),
        ],
        out_specs=(pl.BlockSpec((tm, LANES), lambda i: (i, 0)), pl.BlockSpec((8, LANES), lambda i: (0, 0))),
        scratch_shapes=[pltpu.VMEM((8, LANES), F32)],
        compiler_params=_params(("arbitrary",)),
        name="moe_route",
    )(x, g.reshape(1, d), mod3, mod3, router_p, tri)


def _moe_plan(route, counts, n_tok, tm):
    n_tiles_max = 2 * n_tok // MOE_TM + N_EXPERTS
    cnt = counts[0, :N_EXPERTS].astype(jnp.int32)
    tiles = (cnt + MOE_TM - 1) // MOE_TM
    tile_end = jnp.cumsum(tiles)
    tile_start = tile_end - tiles
    row_start = tile_start * MOE_TM
    n_tiles = tile_end[-1]
    e1 = route[:, R_E1].astype(jnp.int32)
    e2 = route[:, R_E2].astype(jnp.int32)
    pos1 = row_start[e1] + route[:, R_RANK1].astype(jnp.int32)
    pos2 = row_start[e2] + route[:, R_RANK2].astype(jnp.int32)
    shape = (n_tok // tm, 1, tm)
    tile_id = jnp.arange(n_tiles_max, dtype=jnp.int32)
    tile_expert = jnp.minimum(jnp.sum((tile_id[:, None] >= tile_end[None, :]).astype(jnp.int32), axis=1),
                              N_EXPERTS - 1)
    last_tile = jnp.where(tiles > 0, tile_end - 1, -1)
    tail_tile = n_tiles + jnp.arange(N_EXPERTS, dtype=jnp.int32)
    tail_tile = jnp.where(tail_tile < n_tiles_max, tail_tile, -1)
    zero_tiles = jnp.concatenate([last_tile, tail_tile])
    zero_rows = jnp.where(zero_tiles >= 0, zero_tiles * MOE_TM, -1).astype(jnp.int32)
    return pos1.reshape(shape), pos2.reshape(shape), tile_expert.astype(jnp.int32), zero_rows, n_tiles_max


def _moe_dispatch_kernel(zero_ref, pos1_ref, pos2_ref, x_ref, g_ref, sc_ref, sh_ref, xs_ref,
                         h_scr, zero_scr, sem, zsem):
    i = pl.program_id(0)
    tm = x_ref.shape[0]

    @pl.when(i == 0)
    def _():
        zero_scr[...] = jnp.zeros_like(zero_scr)
        for z in range(2 * N_EXPERTS):
            @pl.when(zero_ref[z] >= 0)
            def _():
                row = pl.multiple_of(zero_ref[z], MOE_TM)
                pltpu.make_async_copy(zero_scr, xs_ref.at[pl.ds(row, MOE_TM), :], zsem).start()
        for z in range(2 * N_EXPERTS):
            @pl.when(zero_ref[z] >= 0)
            def _():
                pltpu.make_async_copy(zero_scr, xs_ref.at[pl.ds(0, MOE_TM), :], zsem).wait()

    h_scr[...] = _norm_mod(x_ref[...], g_ref[...], sc_ref[...], sh_ref[...])

    def issue(r, carry):
        src = h_scr.at[pl.ds(r, 1), :]
        pltpu.make_async_copy(src, xs_ref.at[pl.ds(pos1_ref[0, r], 1), :], sem).start()
        pltpu.make_async_copy(src, xs_ref.at[pl.ds(pos2_ref[0, r], 1), :], sem).start()
        return carry

    lax.fori_loop(0, tm, issue, 0, unroll=8)
    for _ in range(2):
        pltpu.make_async_copy(h_scr, xs_ref.at[pl.ds(0, tm), :], sem).wait()


def moe_dispatch(x, mod3, g, pos1, pos2, zero_rows, n_rows, seq, *, tm):
    t, d = x.shape
    tps = seq // tm
    mod_spec = lambda k: pl.BlockSpec((None, 1, d), lambda i, *_: (i // tps, 0, k))
    pos_spec = pl.BlockSpec((None, 1, tm), lambda i, *_: (i, 0, 0), memory_space=pltpu.SMEM)
    return pl.pallas_call(
        _moe_dispatch_kernel,
        out_shape=jax.ShapeDtypeStruct((n_rows, d), F32),
        grid_spec=pltpu.PrefetchScalarGridSpec(
            num_scalar_prefetch=1,
            grid=(t // tm,),
            in_specs=[
                pos_spec, pos_spec,
                pl.BlockSpec((tm, d), lambda i, *_: (i, 0)),
                pl.BlockSpec((1, d), lambda i, *_: (0, 0)),
                mod_spec(4), mod_spec(3),
            ],
            out_specs=pl.BlockSpec(memory_space=pl.ANY),
            scratch_shapes=[pltpu.VMEM((tm, d), F32), pltpu.VMEM((MOE_TM, d), F32),
                            pltpu.SemaphoreType.DMA, pltpu.SemaphoreType.DMA],
        ),
        compiler_params=_params(("arbitrary",)),
        name="moe_dispatch",
    )(zero_rows, pos1, pos2, x, g.reshape(1, d), mod3, mod3)


def _moe_expert_kernel(expert_ref, xs_ref, wgu_ref, wd_ref, ys_ref):
    del expert_ref
    f = wd_ref.shape[0]
    gu = _dot(xs_ref[...].astype(BF16), wgu_ref[...])
    act = _silu_mul(gu[:, :f], gu[:, f:]).astype(BF16)
    ys_ref[...] = _dot(act, wd_ref[...])


def moe_expert(xs, w_gu, w_down, tile_expert):
    n_rows, d = xs.shape
    _, _, two_f = w_gu.shape
    return pl.pallas_call(
        _moe_expert_kernel,
        out_shape=jax.ShapeDtypeStruct((n_rows, d), F32),
        grid_spec=pltpu.PrefetchScalarGridSpec(
            num_scalar_prefetch=1,
            grid=(n_rows // MOE_TM,),
            in_specs=[
                pl.BlockSpec((MOE_TM, d), lambda s, ex: (s, 0)),
                pl.BlockSpec((None, d, two_f), lambda s, ex: (ex[s], 0, 0)),
                pl.BlockSpec((None, two_f // 2, d), lambda s, ex: (ex[s], 0, 0)),
            ],
            out_specs=pl.BlockSpec((MOE_TM, d), lambda s, ex: (s, 0)),
        ),
        compiler_params=_params(("arbitrary",)),
        name="moe_expert",
    )(tile_expert, xs, w_gu, w_down)


def _moe_combine_kernel(pos1_ref, pos2_ref, x_ref, route_ref, gate_ref, ys_ref, *refs, n_a):
    if n_a is None:
        o_ref, y1_scr, y2_scr, sem = refs
    else:
        fg_ref, oa_ref, ob_ref, y1_scr, y2_scr, sem = refs
    tm = x_ref.shape[0]

    def issue(r, carry):
        pltpu.make_async_copy(ys_ref.at[pl.ds(pos1_ref[0, r], 1), :], y1_scr.at[pl.ds(r, 1), :], sem).start()
        pltpu.make_async_copy(ys_ref.at[pl.ds(pos2_ref[0, r], 1), :], y2_scr.at[pl.ds(r, 1), :], sem).start()
        return carry

    lax.fori_loop(0, tm, issue, 0, unroll=8)
    for buf in (y1_scr, y2_scr):
        pltpu.make_async_copy(ys_ref.at[pl.ds(0, tm), :], buf, sem).wait()
    route = route_ref[...]
    w1 = route[:, R_W1:R_W1 + 1]
    w2 = route[:, R_W2:R_W2 + 1]
    x = x_ref[...] + gate_ref[...] * (w1 * y1_scr[...] + w2 * y2_scr[...])
    if n_a is None:
        o_ref[...] = x
        return
    ms = jnp.mean(x * x, axis=-1, keepdims=True)
    y = (x * lax.rsqrt(ms + NORM_EPS)) * fg_ref[...]
    i = pl.program_id(0)

    @pl.when(i < n_a)
    def _():
        oa_ref[...] = y

    @pl.when(i >= n_a)
    def _():
        ob_ref[...] = y


def moe_combine(x, mod3, route, pos1, pos2, ys, seq, *, tm, final=None):
    t, d = x.shape
    tps = seq // tm
    pos_spec = pl.BlockSpec((None, 1, tm), lambda i: (i, 0, 0), memory_space=pltpu.SMEM)
    in_specs = [
        pos_spec, pos_spec,
        pl.BlockSpec((tm, d), lambda i: (i, 0)),
        pl.BlockSpec((tm, LANES), lambda i: (i, 0)),
        pl.BlockSpec((None, 1, d), lambda i: (i // tps, 0, 5)),
        pl.BlockSpec(memory_space=pl.ANY),
    ]
    args = [pos1, pos2, x, route, mod3, ys]
    if final is None:
        n_a = None
        out_shape = jax.ShapeDtypeStruct((t, d), F32)
        out_specs = pl.BlockSpec((tm, d), lambda i: (i, 0))
    else:
        final_g, rows_a = final
        n_a = rows_a // tm
        in_specs.append(pl.BlockSpec((1, d), lambda i: (0, 0)))
        args.append(final_g.reshape(1, d))
        out_shape = (jax.ShapeDtypeStruct((rows_a, d), F32), jax.ShapeDtypeStruct((t - rows_a, d), F32))
        out_specs = (pl.BlockSpec((tm, d), lambda i: (jnp.minimum(i, n_a - 1), 0)),
                     pl.BlockSpec((tm, d), lambda i: (jnp.maximum(i - n_a, 0), 0)))
    return pl.pallas_call(
        functools.partial(_moe_combine_kernel, n_a=n_a),
        out_shape=out_shape,
        grid=(t // tm,),
        in_specs=in_specs,
        out_specs=out_specs,
        scratch_shapes=[pltpu.VMEM((tm, d), F32), pltpu.VMEM((tm, d), F32), pltpu.SemaphoreType.DMA],
        compiler_params=_params(("arbitrary",)),
        name="moe_combine",
    )(*args)


def moe_ffn(x, mod3, g, router, w_gu, w_down, seq, *, tm, final=None):
    t = x.shape[0]
    route, counts = moe_route(x, mod3, g, router, seq, tm=tm)
    pos1, pos2, tile_expert, zero_rows, n_tiles_max = _moe_plan(route, counts, t, tm)
    xs = moe_dispatch(x, mod3, g, pos1, pos2, zero_rows, n_tiles_max * MOE_TM, seq, tm=tm)
    ys = moe_expert(xs, w_gu, w_down, tile_expert)
    return moe_combine(x, mod3, route, pos1, pos2, ys, seq, tm=tm, final=final)


def _lambda_init(layer):
    return 0.8 - 0.6 * math.exp(-0.3 * layer)


def _mixer_a(x, mod3, g, w_in, w_out, lam_p, subln, tables, nb, seq, layer, cfg):
    qkv = inproj(x, mod3, g, w_in, tables, seq, q_cols=D_MODEL, rope_cols=2 * D_MODEL,
                 tm=cfg["tm"], tn=cfg["tn_in"]).reshape(nb * seq, 3 * D_MODEL)
    o = attn_a(qkv, lam_p, subln, nb, seq, _lambda_init(layer), tq=cfg["tq_a"], tk=cfg["tk_a"],
               tk_pv=cfg["tk_pv_a"], unroll=cfg["unroll_a"])
    return outproj(o, w_out, x, mod3, 2, seq, tm=cfg["tm"], tn=cfg["tn_out_mix"])


def _mixer_b(x, mod3, g, w_in, w_out, sink, tables, nb, seq, cfg):
    nq = B_HEADS * HEAD_DIM
    qkv = inproj(x, mod3, g, w_in, tables, seq, q_cols=nq, rope_cols=nq + B_KV_HEADS * HEAD_DIM,
                 tm=cfg["tm"], tn=cfg["tn_in_b"]).reshape(nb, seq, -1)
    o = band_attn(qkv, q_col=0, k_col=4, v_col=5, n_kv=B_KV_HEADS, group=B_HEADS // B_KV_HEADS,
                  half=B_WINDOW, tl=cfg["tl_b"], sink=sink)
    return outproj(o.reshape(nb * seq, nq), w_out, x, mod3, 2, seq, tm=cfg["tm"], tn=cfg["tn_out_mix"])


def _mixer_c(x, mod3, g, w_in, w_out, tables, nb, seq, cfg):
    n_groups = len(C_CONFIGS)
    os_, lses, dils = [], [], []
    for gi, (window, dil) in enumerate(C_CONFIGS):
        qkv = inproj(x, mod3, g, w_in, tables, seq, q_cols=C_WIDTH, rope_cols=2 * C_WIDTH,
                     tm=cfg["tm"], tn=C_WIDTH, dil=dil, col_stride=n_groups, col_off=gi,
                     n_cols=3 * C_WIDTH)
        sub = seq // dil
        o, lse = band_attn(qkv.reshape(nb * dil, sub, 3 * C_WIDTH), q_col=0, k_col=1, v_col=2,
                           n_kv=C_HEADS, group=1, half=window // (2 * dil), tl=cfg["tl_c"],
                           emit_lse=True)
        os_.append(o.reshape(nb, dil, sub, C_WIDTH))
        lses.append(lse.reshape(nb, dil, sub, C_WIDTH))
        dils.append(dil)
    return merge_outproj(os_, lses, dils, w_out, x, mod3, 2, seq, tm=cfg["tm_merge"],
                         tn=cfg["tn_out_ffn"])


def _default_cfg(seq):
    return dict(tm=min(512, seq), tn_in=1024, tn_in_b=512, tn_out_mix=D_MODEL, tn_out_ffn=1024,
                tq_a=min(512, seq), tk_a=min(512, seq), tl_b=min(256, seq),
                tl_c=min(128, seq // C_CONFIGS[-1][1]), tm_merge=min(512, seq), tn_up=1024,
                unroll_a=True, tk_pv_a=min(2048, seq))


def trunk_all(x, c8, nb, seq, ada_w, ada_b, norm_mix, norm_ffn, a_w_in, a_w_out, a_lambda, a_subln,
              b_w_in, b_w_out, b_sink, c_w_in, c_w_out, f_w_gu, f_w_down, moe_router, moe_w_gu,
              moe_w_down, final_norm_g, cfg=None):
    assert (DEPTH - 1) % 2 == 1, "the final norm is fused into the last layer's routed-FFN combine"
    rows_a = x[0].shape[0]
    cfg = cfg or _default_cfg(seq)
    bf = lambda w: w.astype(BF16)
    mod_all = adaln(c8, ada_w, ada_b)
    tables = _rope_tables(seq)
    for i in range(DEPTH):
        mod3 = mod_all[i].reshape(8, 1, 6 * D_MODEL)
        kind, jm = i % 3, i // 3
        if kind == 0:
            x = _mixer_a(x, mod3, norm_mix[i], bf(a_w_in[jm]), bf(a_w_out[jm]), a_lambda[jm],
                         a_subln[jm], tables, nb, seq, i, cfg)
        elif kind == 1:
            x = _mixer_b(x, mod3, norm_mix[i], bf(b_w_in[jm]), bf(b_w_out[jm]), b_sink[jm], tables,
                         nb, seq, cfg)
        else:
            x = _mixer_c(x, mod3, norm_mix[i], bf(c_w_in[jm]), bf(c_w_out[jm]), tables, nb, seq, cfg)
        jf = i // 2
        if i % 2 == 0:
            act = ffn_up(x, mod3, norm_ffn[i], bf(f_w_gu[jf]), seq, tm=cfg["tm"], tn=cfg["tn_up"])
            x = outproj(act, bf(f_w_down[jf]), x, mod3, 5, seq, tm=cfg["tm"], tn=cfg["tn_out_ffn"])
        else:
            x = moe_ffn(x, mod3, norm_ffn[i], moe_router[jf], bf(moe_w_gu[jf]), bf(moe_w_down[jf]), seq,
                        tm=cfg["tm"], final=(final_norm_g, rows_a) if i == DEPTH - 1 else None)
    return x


def kernel(x_prompt, x_sample, c_prompt, c_sample, ada_w, ada_b, norm_mix, norm_ffn, a_w_in, a_w_out,
           a_lambda, a_subln, b_w_in, b_w_out, b_sink, c_w_in, c_w_out, f_w_gu, f_w_down, moe_router,
           moe_w_gu, moe_w_down, final_norm):
    bp, seq, d = x_prompt.shape
    bs = x_sample.shape[0]
    assert x_sample.shape[1] == seq
    nb = bp + bs
    x = (x_prompt.reshape(bp * seq, d), x_sample.reshape(bs * seq, d))
    c8 = jnp.concatenate([c_prompt, c_sample, jnp.zeros((8 - nb, d), F32)], axis=0)
    ya, yb = trunk_all(x, c8, nb, seq, ada_w, ada_b, norm_mix, norm_ffn, a_w_in, a_w_out, a_lambda,
                       a_subln, b_w_in, b_w_out, b_sink, c_w_in, c_w_out, f_w_gu, f_w_down, moe_router,
                       moe_w_gu, moe_w_down, final_norm)
    return (ya.reshape(bp, seq, d), yb.reshape(bs, seq, d))
```

```python
import functools
import math

import jax
import jax.numpy as jnp
from jax import lax
from jax.experimental import pallas as pl
from jax.experimental.pallas import tpu as pltpu

F32 = jnp.float32
BF16 = jnp.bfloat16

D_MODEL = 2048
DEPTH = 4
HEAD_DIM = 128
ROPE_THETA = 500000.0
ROT = HEAD_DIM // 4
NORM_EPS = 1e-6
NEG = -1e30
A_HEADS = 8
B_HEADS = 16
B_KV_HEADS = 4
B_WINDOW = 128
C_CONFIGS = ((128, 1), (512, 4), (2048, 16))
C_HEADS = 8
C_WIDTH = C_HEADS * HEAD_DIM
FFN_DIM = 2 * D_MODEL
MOE_DIM = D_MODEL // 2
N_EXPERTS = 8

LOG2E = math.log2(math.e)
Q_SCALE = HEAD_DIM ** -0.5 * LOG2E

LANES = 128
VMEM_LIMIT = 56 * 1024 * 1024

NT_DIMS = (((1,), (1,)), ((), ()))


def _params(sem, vmem=VMEM_LIMIT):
    return pltpu.CompilerParams(dimension_semantics=sem, vmem_limit_bytes=vmem)


def _dot(a, b):
    return jnp.dot(a, b, preferred_element_type=F32)


def _split_bf16(x):
    hi = x.astype(BF16)
    lo = (x - hi.astype(F32)).astype(BF16)
    return hi, lo


def _dot3(a, b):
    a_hi, a_lo = _split_bf16(a)
    b_hi, b_lo = _split_bf16(b)
    return _dot(a_hi, b_hi) + (_dot(a_hi, b_lo) + _dot(a_lo, b_hi))


def _row_sources(x):
    return tuple(x) if isinstance(x, (tuple, list)) else (x,)


def _row_specs(xs, tm, width, col_of):
    if len(xs) == 1:
        return [pl.BlockSpec((tm, width), lambda i, *a: (i, col_of(*a)))]
    n_a = xs[0].shape[0] // tm
    return [pl.BlockSpec((tm, width), lambda i, *a: (jnp.minimum(i, n_a - 1), col_of(*a))),
            pl.BlockSpec((tm, width), lambda i, *a: (jnp.maximum(i - n_a, 0), col_of(*a)))]


def _read_rows(x_refs, n_a):
    if len(x_refs) == 1:
        return x_refs[0][...]
    return jnp.where(pl.program_id(0) < n_a, x_refs[0][...], x_refs[1][...])


def _norm_mod(x, g, sc, sh):
    ms = jnp.mean(x * x, axis=-1, keepdims=True)
    y = x * lax.rsqrt(ms + NORM_EPS)
    return (y * g) * (1.0 + sc) + sh


def _adaln_kernel(c_ref, w_ref, b_ref, o_ref):
    c = c_ref[...]
    cs = c * jax.nn.sigmoid(c)
    o_ref[...] = _dot3(cs, w_ref[...]) + b_ref[...]


def adaln(c8, ada_w, ada_b, tn=512):
    depth, d, n = ada_w.shape
    return pl.pallas_call(
        _adaln_kernel,
        out_shape=jax.ShapeDtypeStruct((depth, 8, n), F32),
        grid=(depth, n // tn),
        in_specs=[
            pl.BlockSpec((8, d), lambda l, j: (0, 0)),
            pl.BlockSpec((None, d, tn), lambda l, j: (l, 0, j)),
            pl.BlockSpec((None, 1, tn), lambda l, j: (l, 0, j)),
        ],
        out_specs=pl.BlockSpec((None, 8, tn), lambda l, j: (l, 0, j)),
        compiler_params=_params(("arbitrary", "arbitrary")),
        name="adaln",
    )(c8, ada_w, ada_b.reshape(depth, 1, n))


def _rope_tables(seq):
    pos = jnp.arange(seq, dtype=F32)
    half = ROT // 2
    inv = ROPE_THETA ** (-jnp.arange(half, dtype=F32) * (2.0 / ROT))
    ang = pos[:, None] * inv[None, :]
    cos, sin = jnp.cos(ang), jnp.sin(ang)
    zeros = lambda n: jnp.zeros((seq, n), F32)
    cos_t = jnp.concatenate([cos, cos, jnp.ones((seq, HEAD_DIM - ROT), F32)], axis=1)
    sin_a = jnp.concatenate([zeros(half), sin, zeros(HEAD_DIM - ROT)], axis=1)
    sin_b = jnp.concatenate([-sin, zeros(HEAD_DIM - half)], axis=1)
    return cos_t, sin_a, sin_b


def _perm_tables(tables, dil):
    return tuple(t.reshape(-1, dil, HEAD_DIM).transpose(1, 0, 2) for t in tables)


def _inproj_kernel(*refs, n_x, n_a, n_q, n_rope, tn, dil):
    x_refs = refs[:n_x]
    g_ref, sc_ref, sh_ref, w_ref, cos_ref, sa_ref, sb_ref, o_ref, h_scr = refs[n_x:n_x + 9]
    maybe_slab_scr = refs[n_x + 9:]
    j = pl.program_id(1)
    tm, d = x_refs[0].shape
    n = tm // dil

    @pl.when(j == 0)
    def _():
        h = _norm_mod(_read_rows(x_refs, n_a), g_ref[...], sc_ref[...], sh_ref[...])
        if dil == 1:
            h_scr[...] = h.astype(BF16)
        else:
            (slab_scr,) = maybe_slab_scr
            for c in range(d // LANES):
                slab_scr[c] = h[:, c * LANES:(c + 1) * LANES]
            for r in range(dil):
                for c in range(d // LANES):
                    rows = slab_scr[c, pl.ds(r, n, stride=dil), :]
                    h_scr[r * n:(r + 1) * n, c * LANES:(c + 1) * LANES] = rows.astype(BF16)

    @pl.when(j < n_rope)
    def _():
        scale = jnp.where(j < n_q, Q_SCALE, 1.0).astype(F32)
        cos = cos_ref[...].reshape(tm, HEAD_DIM) * scale
        sa = sa_ref[...].reshape(tm, HEAD_DIM) * scale
        sb = sb_ref[...].reshape(tm, HEAD_DIM) * scale
        acc = _dot(h_scr[...], w_ref[...])
        for hh in range(tn // HEAD_DIM):
            a = acc[:, hh * HEAD_DIM:(hh + 1) * HEAD_DIM]
            r = a * cos + pltpu.roll(a, ROT // 2, 1) * sa + pltpu.roll(a, HEAD_DIM - ROT // 2, 1) * sb
            o_ref[:, :, hh * HEAD_DIM:(hh + 1) * HEAD_DIM] = r.reshape(dil, n, HEAD_DIM).astype(BF16)

    @pl.when(j >= n_rope)
    def _():
        o_ref[...] = _dot(h_scr[...], w_ref[...]).reshape(dil, n, tn).astype(BF16)


def inproj(x, mod3, g, w, tables, seq, *, q_cols, rope_cols, tm, tn, dil=1, col_stride=1, col_off=0,
           n_cols=None):
    xs = _row_sources(x)
    t = sum(a.shape[0] for a in xs)
    d = xs[0].shape[1]
    n_cols = n_cols or w.shape[1]
    tps = seq // tm
    n = tm // dil
    kern = functools.partial(_inproj_kernel, n_x=len(xs), n_a=xs[0].shape[0] // tm, n_q=q_cols // tn,
                             n_rope=rope_cols // tn, tn=tn, dil=dil)
    mod_spec = lambda k: pl.BlockSpec((None, 1, d), lambda i, j: (i // tps, 0, k))
    tab_spec = pl.BlockSpec((dil, n, HEAD_DIM), lambda i, j: (0, i % tps, 0))
    return pl.pallas_call(
        kern,
        out_shape=jax.ShapeDtypeStruct((t // seq, dil, seq // dil, n_cols), BF16),
        grid=(t // tm, n_cols // tn),
        in_specs=_row_specs(xs, tm, d, lambda j: 0) + [
            pl.BlockSpec((1, d), lambda i, j: (0, 0)),
            mod_spec(1), mod_spec(0),
            pl.BlockSpec((d, tn), lambda i, j: (0, j * col_stride + col_off)),
            tab_spec, tab_spec, tab_spec,
        ],
        out_specs=pl.BlockSpec((None, dil, n, tn), lambda i, j: (i // tps, 0, i % tps, j)),
        scratch_shapes=[pltpu.VMEM((tm, d), BF16)]
        + ([pltpu.VMEM((d // LANES, tm, LANES), F32)] if dil > 1 else []),
        compiler_params=_params(("parallel", "arbitrary")),
        name="inproj",
    )(*xs, g.reshape(1, d), mod3, mod3, w, *_perm_tables(tables, dil))


L_SAFE = 2.0 ** -64


def _attn_a_kernel(q_ref, k_ref, v_ref, lam_ref, subln_ref, o_ref, p_scr, knorm_scr,
                   *, tk, tk_pv, lam0, unroll):
    i = pl.program_id(2)
    seq = k_ref.shape[0]
    tq = q_ref.shape[0]
    dv = v_ref.shape[1]
    n_chunks = seq // tk

    @pl.when(i == 0)
    def _():
        def body(c, carry):
            off = pl.multiple_of(c * tk, tk)
            kf = k_ref[pl.ds(off, tk), :].astype(F32)
            sq = kf * kf
            n0 = jnp.max(jnp.sum(sq[:, :HEAD_DIM], axis=1, keepdims=True), axis=0, keepdims=True)
            n1 = jnp.max(jnp.sum(sq[:, HEAD_DIM:], axis=1, keepdims=True), axis=0, keepdims=True)
            return jnp.maximum(carry[0], n0), jnp.maximum(carry[1], n1)

        zero = jnp.zeros((1, 1), F32)
        n0, n1 = lax.fori_loop(0, n_chunks, body, (zero, zero))
        knorm_scr[0:1, :] = jnp.broadcast_to(jnp.sqrt(n0), (1, LANES))
        knorm_scr[1:2, :] = jnp.broadcast_to(jnp.sqrt(n1), (1, LANES))

    q = q_ref[...]
    qs = (q[:, :HEAD_DIM], q[:, HEAD_DIM:])
    lp = lam_ref[...].astype(F32)
    lam = (jnp.exp(jnp.sum(lp[0:1] * lp[1:2], axis=1, keepdims=True))
           - jnp.exp(jnp.sum(lp[2:3] * lp[3:4], axis=1, keepdims=True)) + lam0)

    def finish(o):
        ms = jnp.mean(o * o, axis=1, keepdims=True)
        y = (o * lax.rsqrt(ms + NORM_EPS)) * subln_ref[...]
        o_ref[...] = (y * (1.0 - lam0)).astype(BF16)

    ls = []
    for mp in range(2):
        qc = qs[mp]
        qf = qc.astype(F32)
        qn = jnp.sqrt(jnp.sum(qf * qf, axis=1, keepdims=True))
        shift = qn * (knorm_scr[mp:mp + 1, :] * 1.001) + 1.0
        cols = slice(mp * HEAD_DIM, (mp + 1) * HEAD_DIM)

        def chunk(c, lpart, mp=mp, qc=qc, shift=shift, cols=cols):
            off = pl.multiple_of(c * tk, tk)
            s = lax.dot_general(qc, k_ref[pl.ds(off, tk), cols], NT_DIMS, preferred_element_type=F32)
            for g in range(tk // LANES):
                p = jnp.exp2(s[:, g * LANES:(g + 1) * LANES] - shift)
                lpart = lpart + p
                p_scr[mp, :, pl.ds(pl.multiple_of(off + g * LANES, LANES), LANES)] = p
            return lpart

        lpart = lax.fori_loop(0, n_chunks, chunk, jnp.zeros((tq, LANES), F32), unroll=unroll)
        ls.append(jnp.sum(lpart, axis=1, keepdims=True))

    l_min = jnp.min(jnp.minimum(ls[0], ls[1]))

    @pl.when(l_min >= L_SAFE)
    def _():
        ratio = jnp.broadcast_to(lam * ls[0] / ls[1], (tq, LANES))
        acc = jnp.zeros((tq, dv), F32)
        for c in range(seq // tk_pv):
            groups = []
            for g in range(tk_pv // LANES):
                cs = slice(c * tk_pv + g * LANES, c * tk_pv + (g + 1) * LANES)
                groups.append((p_scr[0, :, cs] - ratio * p_scr[1, :, cs]).astype(BF16))
            a = jnp.concatenate(groups, axis=1)
            acc = acc + _dot(a, v_ref[c * tk_pv:(c + 1) * tk_pv, :])
        finish(acc / ls[0])

    @pl.when(l_min < L_SAFE)
    def _():
        def update(qc, kc, v, m, l, acc):
            s = lax.dot_general(qc, kc, NT_DIMS, preferred_element_type=F32)
            m_new = jnp.maximum(m, jnp.max(s, axis=1, keepdims=True))
            alpha = jnp.exp2(m - m_new)
            p = jnp.exp2(s - m_new)
            l = alpha * l + jnp.sum(p, axis=1, keepdims=True)
            acc = alpha * acc + _dot(p.astype(BF16), v)
            return m_new, l, acc

        def body(c, carry):
            m0, l0, a0, m1, l1, a1 = carry
            off = pl.multiple_of(c * tk, tk)
            k = k_ref[pl.ds(off, tk), :]
            v = v_ref[pl.ds(off, tk), :]
            m0, l0, a0 = update(qs[0], k[:, :HEAD_DIM], v, m0, l0, a0)
            m1, l1, a1 = update(qs[1], k[:, HEAD_DIM:], v, m1, l1, a1)
            return m0, l0, a0, m1, l1, a1

        m_init = jnp.full((tq, 1), NEG, F32)
        l_init = jnp.zeros((tq, 1), F32)
        a_init = jnp.zeros((tq, dv), F32)
        _, l0, a0, _, l1, a1 = lax.fori_loop(
            0, n_chunks, body, (m_init, l_init, a_init, m_init, l_init, a_init))
        finish(a0 / l0 - lam * (a1 / l1))


def attn_a(qkv, lam_p, subln, nb, seq, lam0, *, tq, tk, tk_pv, unroll):
    t = qkv.shape[0]
    dv = 2 * HEAD_DIM
    nq = seq // tq
    kern = functools.partial(_attn_a_kernel, tk=tk, tk_pv=tk_pv, lam0=lam0, unroll=unroll)
    return pl.pallas_call(
        kern,
        out_shape=jax.ShapeDtypeStruct((t, A_HEADS * dv), BF16),
        grid=(nb, A_HEADS, nq),
        in_specs=[
            pl.BlockSpec((tq, dv), lambda b, h, i: (b * nq + i, h)),
            pl.BlockSpec((seq, dv), lambda b, h, i: (b, A_HEADS + h), pipeline_mode=pl.Buffered(1)),
            pl.BlockSpec((seq, dv), lambda b, h, i: (b, 2 * A_HEADS + h), pipeline_mode=pl.Buffered(1)),
            pl.BlockSpec((4, HEAD_DIM), lambda b, h, i: (0, 0)),
            pl.BlockSpec((1, dv), lambda b, h, i: (0, 0)),
        ],
        out_specs=pl.BlockSpec((tq, dv), lambda b, h, i: (b * nq + i, h)),
        scratch_shapes=[pltpu.VMEM((2, tq, seq), F32), pltpu.VMEM((8, LANES), F32)],
        compiler_params=_params(("parallel", "parallel", "arbitrary")),
        name="attn_a",
    )(qkv, qkv, qkv, lam_p, subln.reshape(1, dv))


def _band_kernel(*refs, half, tl, seq_len, n_kv, group, has_sink, emit_lse):
    if has_sink:
        sink_ref, refs = refs[0], refs[1:]
    q_ref, kp_ref, kc_ref, kn_ref, vp_ref, vc_ref, vn_ref = refs[:7]
    o_ref = refs[7]
    lse_ref = refs[8] if emit_lse else None
    i = pl.program_id(1)
    win = tl + 2 * half
    qpos = i * tl + lax.broadcasted_iota(jnp.int32, (tl, win), 0)
    kpos = i * tl - half + lax.broadcasted_iota(jnp.int32, (tl, win), 1)
    dist = jnp.abs(kpos - qpos)
    dist = jnp.where(kpos < 0, win, dist)
    dist = jnp.where(kpos >= seq_len, win, dist)
    bias = jnp.where(dist <= half, 0.0, NEG)
    lane = lax.broadcasted_iota(jnp.int32, (tl, LANES), 1)
    lse_all = jnp.zeros((tl, LANES), F32)
    for kv in range(n_kv):
        cs = slice(kv * HEAD_DIM, (kv + 1) * HEAD_DIM)
        kw = jnp.concatenate([kp_ref[:, cs], kc_ref[:, cs], kn_ref[:, cs]], axis=0)
        vw = jnp.concatenate([vp_ref[:, cs], vc_ref[:, cs], vn_ref[:, cs]], axis=0)
        for g in range(group):
            hd = kv * group + g
            hs = slice(hd * HEAD_DIM, (hd + 1) * HEAD_DIM)
            s = lax.dot_general(q_ref[:, hs], kw, NT_DIMS, preferred_element_type=F32) + bias
            m = jnp.max(s, axis=1, keepdims=True)
            if has_sink:
                sk = sink_ref[hd] * LOG2E
                m = jnp.maximum(m, sk)
            e = jnp.exp2(s - m)
            den = jnp.sum(e, axis=1, keepdims=True)
            if has_sink:
                den = den + jnp.exp2(sk - m)
            o = _dot(e.astype(BF16), vw) / den
            o_ref[:, hs] = o.astype(o_ref.dtype)
            if emit_lse:
                lse_all = jnp.where(lane == hd, m + jnp.log2(den), lse_all)
    if emit_lse:
        lse_ref[...] = lse_all


def band_attn(qkv, *, q_col, k_col, v_col, n_kv, group, half, tl, sink=None, emit_lse=False):
    nb, seq_len, _ = qkv.shape
    wq = n_kv * group * HEAD_DIM
    wk = n_kv * HEAD_DIM
    nq = seq_len // tl
    hb = tl // half
    nh = seq_len // half
    kern = functools.partial(_band_kernel, half=half, tl=tl, seq_len=seq_len, n_kv=n_kv, group=group,
                             has_sink=sink is not None, emit_lse=emit_lse)

    def halo(col):
        prev = pl.BlockSpec((None, half, wk), lambda b, i: (b, jnp.maximum(i * hb - 1, 0), col))
        cur = pl.BlockSpec((None, tl, wk), lambda b, i: (b, i, col))
        nxt = pl.BlockSpec((None, half, wk), lambda b, i: (b, jnp.minimum((i + 1) * hb, nh - 1), col))
        return [prev, cur, nxt]

    in_specs = [pl.BlockSpec((None, tl, wq), lambda b, i: (b, i, q_col))] + halo(k_col) + halo(v_col)
    args = [qkv] * 7
    if sink is not None:
        in_specs = [pl.BlockSpec(memory_space=pltpu.SMEM)] + in_specs
        args = [sink] + args
    out_spec = pl.BlockSpec((None, tl, wq), lambda b, i: (b, i, 0))
    o_shape = jax.ShapeDtypeStruct((nb, seq_len, wq), BF16)
    if emit_lse:
        out_shape = (o_shape, jax.ShapeDtypeStruct((nb, seq_len, LANES), F32))
        out_specs = (out_spec, pl.BlockSpec((None, tl, LANES), lambda b, i: (b, i, 0)))
    else:
        out_shape, out_specs = o_shape, out_spec
    return pl.pallas_call(
        kern,
        out_shape=out_shape,
        grid=(nb, nq),
        in_specs=in_specs,
        out_specs=out_specs,
        compiler_params=_params(("parallel", "arbitrary")),
        name="band_attn",
    )(*args)


def _outproj_kernel(a_ref, w_ref, gate_ref, *refs, n_a):
    x_refs, o_ref = refs[:-1], refs[-1]
    o_ref[...] = _read_rows(x_refs, n_a) + gate_ref[...] * _dot(a_ref[...], w_ref[...])


def outproj(a, w, x, mod3, gate_chunk, seq, *, tm, tn):
    t, k = a.shape
    d = w.shape[1]
    tps = seq // tm
    nj = d // tn
    xs = _row_sources(x)
    return pl.pallas_call(
        functools.partial(_outproj_kernel, n_a=xs[0].shape[0] // tm),
        out_shape=jax.ShapeDtypeStruct((t, d), F32),
        grid=(t // tm, nj),
        in_specs=[
            pl.BlockSpec((tm, k), lambda i, j: (i, 0)),
            pl.BlockSpec((k, tn), lambda i, j: (0, j)),
            pl.BlockSpec((None, 1, tn), lambda i, j: (i // tps, 0, gate_chunk * nj + j)),
        ] + _row_specs(xs, tm, tn, lambda j: j),
        out_specs=pl.BlockSpec((tm, tn), lambda i, j: (i, j)),
        compiler_params=_params(("parallel", "arbitrary")),
        name="outproj",
    )(a, w, mod3, *xs)


def _merge_outproj_kernel(*refs, dils):
    n_g = len(dils)
    o_refs, l_refs = refs[:n_g], refs[n_g:2 * n_g]
    w_ref, x_ref, gate_ref, o_ref, a_scr = refs[2 * n_g:2 * n_g + 5]
    un_scr = refs[2 * n_g + 5:]
    j = pl.program_id(1)

    @pl.when(j == 0)
    def _():
        n_heads = a_scr.shape[1] // LANES
        slabs, ls = [], []
        si = 0
        for og_ref, lg_ref, dil in zip(o_refs, l_refs, dils):
            if dil == 1:
                slabs.append([og_ref[0, :, c * LANES:(c + 1) * LANES].astype(F32) for c in range(n_heads)])
                ls.append(lg_ref[0])
            else:
                so, sl = un_scr[si], un_scr[si + 1]
                si += 2
                n = og_ref.shape[1]
                for r in range(dil):
                    sl[pl.ds(r, n, stride=dil), :] = lg_ref[r]
                    for c in range(n_heads):
                        so[c, pl.ds(r, n, stride=dil), :] = og_ref[r, :, c * LANES:(c + 1) * LANES].astype(F32)
                slabs.append([so[c] for c in range(n_heads)])
                ls.append(sl[...])
        m = functools.reduce(jnp.maximum, ls)
        es = [jnp.exp2(l - m) for l in ls]
        den = functools.reduce(lambda a, b: a + b, es)
        alphas = [e / den for e in es]
        for c in range(n_heads):
            merged = functools.reduce(lambda a, b: a + b,
                                      [al[:, c:c + 1] * sg[c] for al, sg in zip(alphas, slabs)])
            a_scr[:, c * LANES:(c + 1) * LANES] = merged.astype(BF16)

    o_ref[...] = x_ref[...] + gate_ref[...] * _dot(a_scr[...], w_ref[...])


def merge_outproj(os_, lses, dils, w, x, mod3, gate_chunk, seq, *, tm, tn):
    k = os_[0].shape[-1]
    t, d = x.shape
    tps = seq // tm
    nj = d // tn
    grp = lambda dil, width: pl.BlockSpec((None, dil, tm // dil, width),
                                          lambda i, j: (i // tps, 0, i % tps, 0))
    un_scratch = []
    for dil in dils:
        if dil != 1:
            un_scratch += [pltpu.VMEM((k // LANES, tm, LANES), F32), pltpu.VMEM((tm, LANES), F32)]
    kern = functools.partial(_merge_outproj_kernel, dils=tuple(dils))
    return pl.pallas_call(
        kern,
        out_shape=jax.ShapeDtypeStruct((t, d), F32),
        grid=(t // tm, nj),
        in_specs=[grp(dil, k) for dil in dils] + [grp(dil, LANES) for dil in dils] + [
            pl.BlockSpec((k, tn), lambda i, j: (0, j)),
            pl.BlockSpec((tm, tn), lambda i, j: (i, j)),
            pl.BlockSpec((None, 1, tn), lambda i, j: (i // tps, 0, gate_chunk * nj + j)),
        ],
        out_specs=pl.BlockSpec((tm, tn), lambda i, j: (i, j)),
        scratch_shapes=[pltpu.VMEM((tm, k), BF16)] + un_scratch,
        compiler_params=_params(("parallel", "arbitrary")),
        name="merge_outproj",
    )(*os_, *lses, w, x, mod3)


def _silu_mul(g, u):
    return (g * jax.nn.sigmoid(g)) * u


def _ffn_up_kernel(x_ref, g_ref, sc_ref, sh_ref, wg_ref, wu_ref, o_ref, h_scr):
    j = pl.program_id(1)

    @pl.when(j == 0)
    def _():
        h_scr[...] = _norm_mod(x_ref[...], g_ref[...], sc_ref[...], sh_ref[...]).astype(BF16)

    h = h_scr[...]
    o_ref[...] = _silu_mul(_dot(h, wg_ref[...]), _dot(h, wu_ref[...])).astype(BF16)


def ffn_up(x, mod3, g, w_gu, seq, *, tm, tn):
    t, d = x.shape
    f = w_gu.shape[1] // 2
    tps = seq // tm
    nj = f // tn
    mod_spec = lambda k: pl.BlockSpec((None, 1, d), lambda i, j: (i // tps, 0, k))
    return pl.pallas_call(
        _ffn_up_kernel,
        out_shape=jax.ShapeDtypeStruct((t, f), BF16),
        grid=(t // tm, nj),
        in_specs=[
            pl.BlockSpec((tm, d), lambda i, j: (i, 0)),
            pl.BlockSpec((1, d), lambda i, j: (0, 0)),
            mod_spec(4), mod_spec(3),
            pl.BlockSpec((d, tn), lambda i, j: (0, j)),
            pl.BlockSpec((d, tn), lambda i, j: (0, nj + j)),
        ],
        out_specs=pl.BlockSpec((tm, tn), lambda i, j: (i, j)),
        scratch_shapes=[pltpu.VMEM((tm, d), BF16)],
        compiler_params=_params(("parallel", "arbitrary")),
        name="ffn_up",
    )(x, g.reshape(1, d), mod3, mod3, w_gu, w_gu)


MOE_TM = 512
R_E1, R_E2, R_W1, R_W2, R_RANK1, R_RANK2 = range(6)


def _moe_route_kernel(x_ref, g_ref, sc_ref, sh_ref, r_ref, tri_ref, route_ref, cnt_ref, cnt_scr):
    i = pl.program_id(0)

    @pl.when(i == 0)
    def _():
        cnt_scr[...] = jnp.zeros_like(cnt_scr)

    h = _norm_mod(x_ref[...], g_ref[...], sc_ref[...], sh_ref[...])
    logits = _dot3(h, r_ref[...])
    lane = lax.broadcasted_iota(jnp.int32, logits.shape, 1).astype(F32)
    logits = jnp.where(lane < N_EXPERTS, logits, NEG)
    v1 = jnp.max(logits, axis=1, keepdims=True)
    i1 = jnp.min(jnp.where(logits == v1, lane, float(LANES)), axis=1, keepdims=True)
    rest = jnp.where(lane == i1, NEG, logits)
    v2 = jnp.max(rest, axis=1, keepdims=True)
    i2 = jnp.min(jnp.where(rest == v2, lane, float(LANES)), axis=1, keepdims=True)
    e2 = jnp.exp(v2 - v1)
    w1 = 1.0 / (1.0 + e2)
    w2 = e2 / (1.0 + e2)
    sel = jnp.where(lane == i1, 1.0, 0.0) + jnp.where(lane == i2, 1.0, 0.0)
    base = cnt_scr[0:1, :]
    rank = _dot(tri_ref[...], sel.astype(BF16)) + base
    r1 = jnp.sum(jnp.where(lane == i1, rank, 0.0), axis=1, keepdims=True)
    r2 = jnp.sum(jnp.where(lane == i2, rank, 0.0), axis=1, keepdims=True)
    slab = jnp.zeros_like(logits)
    for col, val in ((R_E1, i1), (R_E2, i2), (R_W1, w1), (R_W2, w2), (R_RANK1, r1), (R_RANK2, r2)):
        slab = jnp.where(lane == col, val, slab)
    route_ref[...] = slab
    total = base + jnp.sum(sel, axis=0, keepdims=True)
    cnt_scr[0:1, :] = total
    cnt_ref[...] = jnp.broadcast_to(total, cnt_ref.shape)


def moe_route(x, mod3, g, router, seq, *, tm):
    t, d = x.shape
    tps = seq // tm
    mod_spec = lambda k: pl.BlockSpec((None, 1, d), lambda i: (i // tps, 0, k))
    router_p = jnp.pad(router, ((0, 0), (0, LANES - router.shape[1])))
    tri = (lax.broadcasted_iota(jnp.int32, (tm, tm), 1)
           < lax.broadcasted_iota(jnp.int32, (tm, tm), 0)).astype(BF16)
    return pl.pallas_call(
        _moe_route_kernel,
        out_shape=(jax.ShapeDtypeStruct((t, LANES), F32), jax.ShapeDtypeStruct((8, LANES), F32)),
        grid=(t // tm,),
        in_specs=[
            pl.BlockSpec((tm, d), lambda i: (i, 0)),
            pl.BlockSpec((1, d), lambda i: (0, 0)),
            mod_spec(4), mod_spec(3),
            pl.BlockSpec((d, LANES), lambda i: (0, 0)),
            pl.BlockSpec((tm, tm), lambda i: (0, 0)),
        ],
        out_specs=(pl.BlockSpec((tm, LANES), lambda i: (i, 0)), pl.BlockSpec((8, LANES), lambda i: (0, 0))),
        scratch_shapes=[pltpu.VMEM((8, LANES), F32)],
        compiler_params=_params(("arbitrary",)),
        name="moe_route",
    )(x, g.reshape(1, d), mod3, mod3, router_p, tri)


def _moe_plan(route, counts, n_tok, tm):
    n_tiles_max = 2 * n_tok // MOE_TM + N_EXPERTS
    cnt = counts[0, :N_EXPERTS].astype(jnp.int32)
    tiles = (cnt + MOE_TM - 1) // MOE_TM
    tile_end = jnp.cumsum(tiles)
    tile_start = tile_end - tiles
    row_start = tile_start * MOE_TM
    n_tiles = tile_end[-1]
    e1 = route[:, R_E1].astype(jnp.int32)
    e2 = route[:, R_E2].astype(jnp.int32)
    pos1 = row_start[e1] + route[:, R_RANK1].astype(jnp.int32)
    pos2 = row_start[e2] + route[:, R_RANK2].astype(jnp.int32)
    shape = (n_tok // tm, 1, tm)
    tile_id = jnp.arange(n_tiles_max, dtype=jnp.int32)
    tile_expert = jnp.minimum(jnp.sum((tile_id[:, None] >= tile_end[None, :]).astype(jnp.int32), axis=1),
                              N_EXPERTS - 1)
    last_tile = jnp.where(tiles > 0, tile_end - 1, -1)
    tail_tile = n_tiles + jnp.arange(N_EXPERTS, dtype=jnp.int32)
    tail_tile = jnp.where(tail_tile < n_tiles_max, tail_tile, -1)
    zero_tiles = jnp.concatenate([last_tile, tail_tile])
    zero_rows = jnp.where(zero_tiles >= 0, zero_tiles * MOE_TM, -1).astype(jnp.int32)
    return pos1.reshape(shape), pos2.reshape(shape), tile_expert.astype(jnp.int32), zero_rows, n_tiles_max


def _moe_dispatch_kernel(zero_ref, pos1_ref, pos2_ref, x_ref, g_ref, sc_ref, sh_ref, xs_ref,
                         h_scr, zero_scr, sem, zsem):
    i = pl.program_id(0)
    tm = x_ref.shape[0]

    @pl.when(i == 0)
    def _():
        zero_scr[...] = jnp.zeros_like(zero_scr)
        for z in range(2 * N_EXPERTS):
            @pl.when(zero_ref[z] >= 0)
            def _():
                row = pl.multiple_of(zero_ref[z], MOE_TM)
                pltpu.make_async_copy(zero_scr, xs_ref.at[pl.ds(row, MOE_TM), :], zsem).start()
        for z in range(2 * N_EXPERTS):
            @pl.when(zero_ref[z] >= 0)
            def _():
                pltpu.make_async_copy(zero_scr, xs_ref.at[pl.ds(0, MOE_TM), :], zsem).wait()

    h_scr[...] = _norm_mod(x_ref[...], g_ref[...], sc_ref[...], sh_ref[...])

    def issue(r, carry):
        src = h_scr.at[pl.ds(r, 1), :]
        pltpu.make_async_copy(src, xs_ref.at[pl.ds(pos1_ref[0, r], 1), :], sem).start()
        pltpu.make_async_copy(src, xs_ref.at[pl.ds(pos2_ref[0, r], 1), :], sem).start()
        return carry

    lax.fori_loop(0, tm, issue, 0, unroll=8)
    for _ in range(2):
        pltpu.make_async_copy(h_scr, xs_ref.at[pl.ds(0, tm), :], sem).wait()


def moe_dispatch(x, mod3, g, pos1, pos2, zero_rows, n_rows, seq, *, tm):
    t, d = x.shape
    tps = seq // tm
    mod_spec = lambda k: pl.BlockSpec((None, 1, d), lambda i, *_: (i // tps, 0, k))
    pos_spec = pl.BlockSpec((None, 1, tm), lambda i, *_: (i, 0, 0), memory_space=pltpu.SMEM)
    return pl.pallas_call(
        _moe_dispatch_kernel,
        out_shape=jax.ShapeDtypeStruct((n_rows, d), F32),
        grid_spec=pltpu.PrefetchScalarGridSpec(
            num_scalar_prefetch=1,
            grid=(t // tm,),
            in_specs=[
                pos_spec, pos_spec,
                pl.BlockSpec((tm, d), lambda i, *_: (i, 0)),
                pl.BlockSpec((1, d), lambda i, *_: (0, 0)),
                mod_spec(4), mod_spec(3),
            ],
            out_specs=pl.BlockSpec(memory_space=pl.ANY),
            scratch_shapes=[pltpu.VMEM((tm, d), F32), pltpu.VMEM((MOE_TM, d), F32),
                            pltpu.SemaphoreType.DMA, pltpu.SemaphoreType.DMA],
        ),
        compiler_params=_params(("arbitrary",)),
        name="moe_dispatch",
    )(zero_rows, pos1, pos2, x, g.reshape(1, d), mod3, mod3)


def _moe_expert_kernel(expert_ref, xs_ref, wgu_ref, wd_ref, ys_ref):
    del expert_ref
    f = wd_ref.shape[0]
    gu = _dot(xs_ref[...].astype(BF16), wgu_ref[...])
    act = _silu_mul(gu[:, :f], gu[:, f:]).astype(BF16)
    ys_ref[...] = _dot(act, wd_ref[...])


def moe_expert(xs, w_gu, w_down, tile_expert):
    n_rows, d = xs.shape
    _, _, two_f = w_gu.shape
    return pl.pallas_call(
        _moe_expert_kernel,
        out_shape=jax.ShapeDtypeStruct((n_rows, d), F32),
        grid_spec=pltpu.PrefetchScalarGridSpec(
            num_scalar_prefetch=1,
            grid=(n_rows // MOE_TM,),
            in_specs=[
                pl.BlockSpec((MOE_TM, d), lambda s, ex: (s, 0)),
                pl.BlockSpec((None, d, two_f), lambda s, ex: (ex[s], 0, 0)),
                pl.BlockSpec((None, two_f // 2, d), lambda s, ex: (ex[s], 0, 0)),
            ],
            out_specs=pl.BlockSpec((MOE_TM, d), lambda s, ex: (s, 0)),
        ),
        compiler_params=_params(("arbitrary",)),
        name="moe_expert",
    )(tile_expert, xs, w_gu, w_down)


def _moe_combine_kernel(pos1_ref, pos2_ref, x_ref, route_ref, gate_ref, ys_ref, *refs, n_a):
    if n_a is None:
        o_ref, y1_scr, y2_scr, sem = refs
    else:
        fg_ref, oa_ref, ob_ref, y1_scr, y2_scr, sem = refs
    tm = x_ref.shape[0]

    def issue(r, carry):
        pltpu.make_async_copy(ys_ref.at[pl.ds(pos1_ref[0, r], 1), :], y1_scr.at[pl.ds(r, 1), :], sem).start()
        pltpu.make_async_copy(ys_ref.at[pl.ds(pos2_ref[0, r], 1), :], y2_scr.at[pl.ds(r, 1), :], sem).start()
        return carry

    lax.fori_loop(0, tm, issue, 0, unroll=8)
    for buf in (y1_scr, y2_scr):
        pltpu.make_async_copy(ys_ref.at[pl.ds(0, tm), :], buf, sem).wait()
    route = route_ref[...]
    w1 = route[:, R_W1:R_W1 + 1]
    w2 = route[:, R_W2:R_W2 + 1]
    x = x_ref[...] + gate_ref[...] * (w1 * y1_scr[...] + w2 * y2_scr[...])
    if n_a is None:
        o_ref[...] = x
        return
    ms = jnp.mean(x * x, axis=-1, keepdims=True)
    y = (x * lax.rsqrt(ms + NORM_EPS)) * fg_ref[...]
    i = pl.program_id(0)

    @pl.when(i < n_a)
    def _():
        oa_ref[...] = y

    @pl.when(i >= n_a)
    def _():
        ob_ref[...] = y


def moe_combine(x, mod3, route, pos1, pos2, ys, seq, *, tm, final=None):
    t, d = x.shape
    tps = seq // tm
    pos_spec = pl.BlockSpec((None, 1, tm), lambda i: (i, 0, 0), memory_space=pltpu.SMEM)
    in_specs = [
        pos_spec, pos_spec,
        pl.BlockSpec((tm, d), lambda i: (i, 0)),
        pl.BlockSpec((tm, LANES), lambda i: (i, 0)),
        pl.BlockSpec((None, 1, d), lambda i: (i // tps, 0, 5)),
        pl.BlockSpec(memory_space=pl.ANY),
    ]
    args = [pos1, pos2, x, route, mod3, ys]
    if final is None:
        n_a = None
        out_shape = jax.ShapeDtypeStruct((t, d), F32)
        out_specs = pl.BlockSpec((tm, d), lambda i: (i, 0))
    else:
        final_g, rows_a = final
        n_a = rows_a // tm
        in_specs.append(pl.BlockSpec((1, d), lambda i: (0, 0)))
        args.append(final_g.reshape(1, d))
        out_shape = (jax.ShapeDtypeStruct((rows_a, d), F32), jax.ShapeDtypeStruct((t - rows_a, d), F32))
        out_specs = (pl.BlockSpec((tm, d), lambda i: (jnp.minimum(i, n_a - 1), 0)),
                     pl.BlockSpec((tm, d), lambda i: (jnp.maximum(i - n_a, 0), 0)))
    return pl.pallas_call(
        functools.partial(_moe_combine_kernel, n_a=n_a),
        out_shape=out_shape,
        grid=(t // tm,),
        in_specs=in_specs,
        out_specs=out_specs,
        scratch_shapes=[pltpu.VMEM((tm, d), F32), pltpu.VMEM((tm, d), F32), pltpu.SemaphoreType.DMA],
        compiler_params=_params(("arbitrary",)),
        name="moe_combine",
    )(*args)


def moe_ffn(x, mod3, g, router, w_gu, w_down, seq, *, tm, final=None):
    t = x.shape[0]
    route, counts = moe_route(x, mod3, g, router, seq, tm=tm)
    pos1, pos2, tile_expert, zero_rows, n_tiles_max = _moe_plan(route, counts, t, tm)
    xs = moe_dispatch(x, mod3, g, pos1, pos2, zero_rows, n_tiles_max * MOE_TM, seq, tm=tm)
    ys = moe_expert(xs, w_gu, w_down, tile_expert)
    return moe_combine(x, mod3, route, pos1, pos2, ys, seq, tm=tm, final=final)


def _lambda_init(layer):
    return 0.8 - 0.6 * math.exp(-0.3 * layer)


def _mixer_a(x, mod3, g, w_in, w_out, lam_p, subln, tables, nb, seq, layer, cfg):
    tm_in = cfg["tm"] if len(_row_sources(x)) > 1 else cfg["tm_in"]
    qkv = inproj(x, mod3, g, w_in, tables, seq, q_cols=D_MODEL, rope_cols=2 * D_MODEL,
                 tm=tm_in, tn=cfg["tn_in"]).reshape(nb * seq, 3 * D_MODEL)
    o = attn_a(qkv, lam_p, subln, nb, seq, _lambda_init(layer), tq=cfg["tq_a"], tk=cfg["tk_a"],
               tk_pv=cfg["tk_pv_a"], unroll=cfg["unroll_a"])
    return outproj(o, w_out, x, mod3, 2, seq, tm=cfg["tm"], tn=cfg["tn_out_mix"])


def _mixer_b(x, mod3, g, w_in, w_out, sink, tables, nb, seq, cfg):
    nq = B_HEADS * HEAD_DIM
    qkv = inproj(x, mod3, g, w_in, tables, seq, q_cols=nq, rope_cols=nq + B_KV_HEADS * HEAD_DIM,
                 tm=cfg["tm_in"], tn=cfg["tn_in_b"]).reshape(nb, seq, -1)
    o = band_attn(qkv, q_col=0, k_col=4, v_col=5, n_kv=B_KV_HEADS, group=B_HEADS // B_KV_HEADS,
                  half=B_WINDOW, tl=cfg["tl_b"], sink=sink)
    return outproj(o.reshape(nb * seq, nq), w_out, x, mod3, 2, seq, tm=cfg["tm"], tn=cfg["tn_out_mix"])


def _mixer_c(x, mod3, g, w_in, w_out, tables, nb, seq, cfg):
    n_groups = len(C_CONFIGS)
    os_, lses, dils = [], [], []
    for gi, (window, dil) in enumerate(C_CONFIGS):
        qkv = inproj(x, mod3, g, w_in, tables, seq, q_cols=C_WIDTH, rope_cols=2 * C_WIDTH,
                     tm=cfg["tm_in"], tn=C_WIDTH, dil=dil, col_stride=n_groups, col_off=gi,
                     n_cols=3 * C_WIDTH)
        sub = seq // dil
        o, lse = band_attn(qkv.reshape(nb * dil, sub, 3 * C_WIDTH), q_col=0, k_col=1, v_col=2,
                           n_kv=C_HEADS, group=1, half=window // (2 * dil), tl=cfg["tl_c"],
                           emit_lse=True)
        os_.append(o.reshape(nb, dil, sub, C_WIDTH))
        lses.append(lse.reshape(nb, dil, sub, LANES))
        dils.append(dil)
    return merge_outproj(os_, lses, dils, w_out, x, mod3, 2, seq, tm=cfg["tm_merge"],
                         tn=cfg["tn_out_ffn"])


def _default_cfg(seq):
    return dict(tm=min(512, seq), tm_in=min(1024, seq), tm_up=min(1024, seq), tn_in=1024, tn_in_b=512,
                tn_out_mix=D_MODEL, tn_out_ffn=1024,
                tq_a=min(512, seq), tk_a=min(512, seq), tl_b=min(256, seq),
                tl_c=min(128, seq // C_CONFIGS[-1][1]), tm_merge=min(512, seq), tn_up=512,
                unroll_a=True, tk_pv_a=min(2048, seq))


def trunk_all(x, c8, nb, seq, ada_w, ada_b, norm_mix, norm_ffn, a_w_in, a_w_out, a_lambda, a_subln,
              b_w_in, b_w_out, b_sink, c_w_in, c_w_out, f_w_gu, f_w_down, moe_router, moe_w_gu,
              moe_w_down, final_norm_g, cfg=None):
    assert (DEPTH - 1) % 2 == 1, "the final norm is fused into the last layer's routed-FFN combine"
    rows_a = x[0].shape[0]
    cfg = cfg or _default_cfg(seq)
    bf = lambda w: w.astype(BF16)
    mod_all = adaln(c8, ada_w, ada_b)
    tables = _rope_tables(seq)
    for i in range(DEPTH):
        mod3 = mod_all[i].reshape(8, 1, 6 * D_MODEL)
        kind, jm = i % 3, i // 3
        if kind == 0:
            x = _mixer_a(x, mod3, norm_mix[i], bf(a_w_in[jm]), bf(a_w_out[jm]), a_lambda[jm],
                         a_subln[jm], tables, nb, seq, i, cfg)
        elif kind == 1:
            x = _mixer_b(x, mod3, norm_mix[i], bf(b_w_in[jm]), bf(b_w_out[jm]), b_sink[jm], tables,
                         nb, seq, cfg)
        else:
            x = _mixer_c(x, mod3, norm_mix[i], bf(c_w_in[jm]), bf(c_w_out[jm]), tables, nb, seq, cfg)
        jf = i // 2
        if i % 2 == 0:
            act = ffn_up(x, mod3, norm_ffn[i], bf(f_w_gu[jf]), seq, tm=cfg["tm_up"], tn=cfg["tn_up"])
            x = outproj(act, bf(f_w_down[jf]), x, mod3, 5, seq, tm=cfg["tm"], tn=cfg["tn_out_ffn"])
        else:
            x = moe_ffn(x, mod3, norm_ffn[i], moe_router[jf], bf(moe_w_gu[jf]), bf(moe_w_down[jf]), seq,
                        tm=cfg["tm"], final=(final_norm_g, rows_a) if i == DEPTH - 1 else None)
    return x


def kernel(x_prompt, x_sample, c_prompt, c_sample, ada_w, ada_b, norm_mix, norm_ffn, a_w_in, a_w_out,
           a_lambda, a_subln, b_w_in, b_w_out, b_sink, c_w_in, c_w_out, f_w_gu, f_w_down, moe_router,
           moe_w_gu, moe_w_down, final_norm):
    bp, seq, d = x_prompt.shape
    bs = x_sample.shape[0]
    assert x_sample.shape[1] == seq
    nb = bp + bs
    x = (x_prompt.reshape(bp * seq, d), x_sample.reshape(bs * seq, d))
    c8 = jnp.concatenate([c_prompt, c_sample, jnp.zeros((8 - nb, d), F32)], axis=0)
    ya, yb = trunk_all(x, c8, nb, seq, ada_w, ada_b, norm_mix, norm_ffn, a_w_in, a_w_out, a_lambda,
                       a_subln, b_w_in, b_w_out, b_sink, c_w_in, c_w_out, f_w_gu, f_w_down, moe_router,
                       moe_w_gu, moe_w_down, final_norm)
    return (ya.reshape(bp, seq, d), yb.reshape(bs, seq, d))
```

```python
import functools
import math

import jax
import jax.numpy as jnp
from jax import lax
from jax.experimental import pallas as pl
from jax.experimental.pallas import tpu as pltpu

F32 = jnp.float32
BF16 = jnp.bfloat16

D_MODEL = 2048
DEPTH = 4
HEAD_DIM = 128
ROPE_THETA = 500000.0
ROT = HEAD_DIM // 4
NORM_EPS = 1e-6
NEG = -1e30
A_HEADS = 8
B_HEADS = 16
B_KV_HEADS = 4
B_WINDOW = 128
C_CONFIGS = ((128, 1), (512, 4), (2048, 16))
C_HEADS = 8
C_WIDTH = C_HEADS * HEAD_DIM
FFN_DIM = 2 * D_MODEL
MOE_DIM = D_MODEL // 2
N_EXPERTS = 8

LOG2E = math.log2(math.e)
Q_SCALE = HEAD_DIM ** -0.5 * LOG2E

LANES = 128
VMEM_LIMIT = 56 * 1024 * 1024
NORM_CHUNKS = 4

NT_DIMS = (((1,), (1,)), ((), ()))


def _params(sem, vmem=VMEM_LIMIT):
    return pltpu.CompilerParams(dimension_semantics=sem, vmem_limit_bytes=vmem)


def _dot(a, b):
    return jnp.dot(a, b, preferred_element_type=F32)


def _split_bf16(x):
    hi = x.astype(BF16)
    lo = (x - hi.astype(F32)).astype(BF16)
    return hi, lo


def _dot3(a, b):
    a_hi, a_lo = _split_bf16(a)
    b_hi, b_lo = _split_bf16(b)
    return _dot(a_hi, b_hi) + (_dot(a_hi, b_lo) + _dot(a_lo, b_hi))


def _row_sources(x):
    return tuple(x) if isinstance(x, (tuple, list)) else (x,)


def _row_specs(xs, tm, width, col_of):
    if len(xs) == 1:
        return [pl.BlockSpec((tm, width), lambda i, *a: (i, col_of(*a)))]
    n_a = xs[0].shape[0] // tm
    return [pl.BlockSpec((tm, width), lambda i, *a: (jnp.minimum(i, n_a - 1), col_of(*a))),
            pl.BlockSpec((tm, width), lambda i, *a: (jnp.maximum(i - n_a, 0), col_of(*a)))]


def _read_rows(x_refs, n_a, rs=slice(None)):
    if len(x_refs) == 1:
        return x_refs[0][rs, :]
    return jnp.where(pl.program_id(0) < n_a, x_refs[0][rs, :], x_refs[1][rs, :])


def _norm_mod(x, g, sc, sh):
    ms = jnp.mean(x * x, axis=-1, keepdims=True)
    y = x * lax.rsqrt(ms + NORM_EPS)
    return (y * g) * (1.0 + sc) + sh


def _adaln_kernel(c_ref, w_ref, b_ref, o_ref):
    c = c_ref[...]
    cs = c * jax.nn.sigmoid(c)
    o_ref[...] = _dot3(cs, w_ref[...]) + b_ref[...]


def adaln(c8, ada_w, ada_b, tn=512):
    depth, d, n = ada_w.shape
    return pl.pallas_call(
        _adaln_kernel,
        out_shape=jax.ShapeDtypeStruct((depth, 8, n), F32),
        grid=(depth, n // tn),
        in_specs=[
            pl.BlockSpec((8, d), lambda l, j: (0, 0)),
            pl.BlockSpec((None, d, tn), lambda l, j: (l, 0, j)),
            pl.BlockSpec((None, 1, tn), lambda l, j: (l, 0, j)),
        ],
        out_specs=pl.BlockSpec((None, 8, tn), lambda l, j: (l, 0, j)),
        compiler_params=_params(("arbitrary", "arbitrary")),
        name="adaln",
    )(c8, ada_w, ada_b.reshape(depth, 1, n))


def _rope_tables(seq):
    pos = jnp.arange(seq, dtype=F32)
    half = ROT // 2
    inv = ROPE_THETA ** (-jnp.arange(half, dtype=F32) * (2.0 / ROT))
    ang = pos[:, None] * inv[None, :]
    cos, sin = jnp.cos(ang), jnp.sin(ang)
    zeros = lambda n: jnp.zeros((seq, n), F32)
    cos_t = jnp.concatenate([cos, cos, jnp.ones((seq, HEAD_DIM - ROT), F32)], axis=1)
    sin_a = jnp.concatenate([zeros(half), sin, zeros(HEAD_DIM - ROT)], axis=1)
    sin_b = jnp.concatenate([-sin, zeros(HEAD_DIM - half)], axis=1)
    return cos_t, sin_a, sin_b


def _perm_tables(tables, dil):
    return tuple(t.reshape(-1, dil, HEAD_DIM).transpose(1, 0, 2) for t in tables)


def _inproj_kernel(*refs, n_x, n_a, n_q, n_rope, tn, dil):
    x_refs = refs[:n_x]
    g_ref, sc_ref, sh_ref, w_ref, cos_ref, sa_ref, sb_ref, o_ref, h_scr = refs[n_x:n_x + 9]
    maybe_slab_scr = refs[n_x + 9:]
    j = pl.program_id(1)
    tm, d = x_refs[0].shape
    n = tm // dil
    assert n_q >= 1

    def project(h_rows, rr, ll, scale):
        shape = (rr.stop - rr.start, ll.stop - ll.start, HEAD_DIM)
        cos, sa, sb = (ref[rr, ll, :].reshape(-1, HEAD_DIM) * scale for ref in (cos_ref, sa_ref, sb_ref))
        acc = _dot(h_rows, w_ref[...])
        for hh in range(tn // HEAD_DIM):
            cols = slice(hh * HEAD_DIM, (hh + 1) * HEAD_DIM)
            a = acc[:, cols]
            r = a * cos + pltpu.roll(a, ROT // 2, 1) * sa + pltpu.roll(a, HEAD_DIM - ROT // 2, 1) * sb
            o_ref[rr, ll, cols] = r.reshape(shape).astype(BF16)

    whole = (slice(0, dil), slice(0, n))

    @pl.when(j == 0)
    def _():
        g, sc, sh = g_ref[...], sc_ref[...], sh_ref[...]
        if dil == 1:
            rc = tm // NORM_CHUNKS
            for c in range(NORM_CHUNKS):
                rs = slice(c * rc, (c + 1) * rc)
                hb = _norm_mod(_read_rows(x_refs, n_a, rs), g, sc, sh).astype(BF16)
                h_scr[rs, :] = hb
                project(hb, slice(0, 1), rs, Q_SCALE)
        else:
            h = _norm_mod(_read_rows(x_refs, n_a), g, sc, sh)
            (slab_scr,) = maybe_slab_scr
            for c in range(d // LANES):
                slab_scr[c] = h[:, c * LANES:(c + 1) * LANES]
            per = dil // NORM_CHUNKS
            for k in range(NORM_CHUNKS):
                for r in range(k * per, (k + 1) * per):
                    for c in range(d // LANES):
                        rows = slab_scr[c, pl.ds(r, n, stride=dil), :]
                        h_scr[r * n:(r + 1) * n, c * LANES:(c + 1) * LANES] = rows.astype(BF16)
                project(h_scr[k * per * n:(k + 1) * per * n, :], slice(k * per, (k + 1) * per),
                        slice(0, n), Q_SCALE)

    @pl.when(jnp.logical_and(j > 0, j < n_rope))
    def _():
        project(h_scr[...], *whole, jnp.where(j < n_q, Q_SCALE, 1.0).astype(F32))

    @pl.when(j >= n_rope)
    def _():
        o_ref[...] = _dot(h_scr[...], w_ref[...]).reshape(dil, n, tn).astype(BF16)


def inproj(x, mod3, g, w, tables, seq, *, q_cols, rope_cols, tm, tn, dil=1, col_stride=1, col_off=0,
           n_cols=None):
    xs = _row_sources(x)
    t = sum(a.shape[0] for a in xs)
    d = xs[0].shape[1]
    n_cols = n_cols or w.shape[1]
    tps = seq // tm
    n = tm // dil
    kern = functools.partial(_inproj_kernel, n_x=len(xs), n_a=xs[0].shape[0] // tm, n_q=q_cols // tn,
                             n_rope=rope_cols // tn, tn=tn, dil=dil)
    mod_spec = lambda k: pl.BlockSpec((None, 1, d), lambda i, j: (i // tps, 0, k))
    tab_spec = pl.BlockSpec((dil, n, HEAD_DIM), lambda i, j: (0, i % tps, 0))
    return pl.pallas_call(
        kern,
        out_shape=jax.ShapeDtypeStruct((t // seq, dil, seq // dil, n_cols), BF16),
        grid=(t // tm, n_cols // tn),
        in_specs=_row_specs(xs, tm, d, lambda j: 0) + [
            pl.BlockSpec((1, d), lambda i, j: (0, 0)),
            mod_spec(1), mod_spec(0),
            pl.BlockSpec((d, tn), lambda i, j: (0, j * col_stride + col_off)),
            tab_spec, tab_spec, tab_spec,
        ],
        out_specs=pl.BlockSpec((None, dil, n, tn), lambda i, j: (i // tps, 0, i % tps, j)),
        scratch_shapes=[pltpu.VMEM((tm, d), BF16)]
        + ([pltpu.VMEM((d // LANES, tm, LANES), F32)] if dil > 1 else []),
        compiler_params=_params(("parallel", "arbitrary")),
        name="inproj",
    )(*xs, g.reshape(1, d), mod3, mod3, w, *_perm_tables(tables, dil))


L_SAFE = 2.0 ** -64


def _attn_a_kernel(q_ref, k_ref, v_ref, lam_ref, subln_ref, o_ref, p_scr, knorm_scr,
                   *, tk, tk_pv, lam0, unroll):
    i = pl.program_id(2)
    seq = k_ref.shape[0]
    tq = q_ref.shape[0]
    dv = v_ref.shape[1]
    n_chunks = seq // tk

    @pl.when(i == 0)
    def _():
        def body(c, carry):
            off = pl.multiple_of(c * tk, tk)
            kf = k_ref[pl.ds(off, tk), :].astype(F32)
            sq = kf * kf
            n0 = jnp.max(jnp.sum(sq[:, :HEAD_DIM], axis=1, keepdims=True), axis=0, keepdims=True)
            n1 = jnp.max(jnp.sum(sq[:, HEAD_DIM:], axis=1, keepdims=True), axis=0, keepdims=True)
            return jnp.maximum(carry[0], n0), jnp.maximum(carry[1], n1)

        zero = jnp.zeros((1, 1), F32)
        n0, n1 = lax.fori_loop(0, n_chunks, body, (zero, zero))
        knorm_scr[0:1, :] = jnp.broadcast_to(jnp.sqrt(n0), (1, LANES))
        knorm_scr[1:2, :] = jnp.broadcast_to(jnp.sqrt(n1), (1, LANES))

    q = q_ref[...]
    qs = (q[:, :HEAD_DIM], q[:, HEAD_DIM:])
    lp = lam_ref[...].astype(F32)
    lam = (jnp.exp(jnp.sum(lp[0:1] * lp[1:2], axis=1, keepdims=True))
           - jnp.exp(jnp.sum(lp[2:3] * lp[3:4], axis=1, keepdims=True)) + lam0)

    def finish(o):
        ms = jnp.mean(o * o, axis=1, keepdims=True)
        y = (o * lax.rsqrt(ms + NORM_EPS)) * subln_ref[...]
        o_ref[...] = (y * (1.0 - lam0)).astype(BF16)

    ls = []
    for mp in range(2):
        qc = qs[mp]
        qf = qc.astype(F32)
        qn = jnp.sqrt(jnp.sum(qf * qf, axis=1, keepdims=True))
        shift = qn * (knorm_scr[mp:mp + 1, :] * 1.001) + 1.0
        cols = slice(mp * HEAD_DIM, (mp + 1) * HEAD_DIM)

        def chunk(c, lpart, mp=mp, qc=qc, shift=shift, cols=cols):
            off = pl.multiple_of(c * tk, tk)
            s = lax.dot_general(qc, k_ref[pl.ds(off, tk), cols], NT_DIMS, preferred_element_type=F32)
            for g in range(tk // LANES):
                p = jnp.exp2(s[:, g * LANES:(g + 1) * LANES] - shift)
                lpart = lpart + p
                p_scr[mp, :, pl.ds(pl.multiple_of(off + g * LANES, LANES), LANES)] = p
            return lpart

        lpart = lax.fori_loop(0, n_chunks, chunk, jnp.zeros((tq, LANES), F32), unroll=unroll)
        ls.append(jnp.sum(lpart, axis=1, keepdims=True))

    l_min = jnp.min(jnp.minimum(ls[0], ls[1]))

    @pl.when(l_min >= L_SAFE)
    def _():
        ratio = jnp.broadcast_to(lam * ls[0] / ls[1], (tq, LANES))
        acc = jnp.zeros((tq, dv), F32)
        for c in range(seq // tk_pv):
            groups = []
            for g in range(tk_pv // LANES):
                cs = slice(c * tk_pv + g * LANES, c * tk_pv + (g + 1) * LANES)
                groups.append((p_scr[0, :, cs] - ratio * p_scr[1, :, cs]).astype(BF16))
            a = jnp.concatenate(groups, axis=1)
            acc = acc + _dot(a, v_ref[c * tk_pv:(c + 1) * tk_pv, :])
        finish(acc / ls[0])

    @pl.when(l_min < L_SAFE)
    def _():
        def update(qc, kc, v, m, l, acc):
            s = lax.dot_general(qc, kc, NT_DIMS, preferred_element_type=F32)
            m_new = jnp.maximum(m, jnp.max(s, axis=1, keepdims=True))
            alpha = jnp.exp2(m - m_new)
            p = jnp.exp2(s - m_new)
            l = alpha * l + jnp.sum(p, axis=1, keepdims=True)
            acc = alpha * acc + _dot(p.astype(BF16), v)
            return m_new, l, acc

        def body(c, carry):
            m0, l0, a0, m1, l1, a1 = carry
            off = pl.multiple_of(c * tk, tk)
            k = k_ref[pl.ds(off, tk), :]
            v = v_ref[pl.ds(off, tk), :]
            m0, l0, a0 = update(qs[0], k[:, :HEAD_DIM], v, m0, l0, a0)
            m1, l1, a1 = update(qs[1], k[:, HEAD_DIM:], v, m1, l1, a1)
            return m0, l0, a0, m1, l1, a1

        m_init = jnp.full((tq, 1), NEG, F32)
        l_init = jnp.zeros((tq, 1), F32)
        a_init = jnp.zeros((tq, dv), F32)
        _, l0, a0, _, l1, a1 = lax.fori_loop(
            0, n_chunks, body, (m_init, l_init, a_init, m_init, l_init, a_init))
        finish(a0 / l0 - lam * (a1 / l1))


def attn_a(qkv, lam_p, subln, nb, seq, lam0, *, tq, tk, tk_pv, unroll):
    t = qkv.shape[0]
    dv = 2 * HEAD_DIM
    nq = seq // tq
    kern = functools.partial(_attn_a_kernel, tk=tk, tk_pv=tk_pv, lam0=lam0, unroll=unroll)
    return pl.pallas_call(
        kern,
        out_shape=jax.ShapeDtypeStruct((t, A_HEADS * dv), BF16),
        grid=(nb, A_HEADS, nq),
        in_specs=[
            pl.BlockSpec((tq, dv), lambda b, h, i: (b * nq + i, h)),
            pl.BlockSpec((seq, dv), lambda b, h, i: (b, A_HEADS + h), pipeline_mode=pl.Buffered(1)),
            pl.BlockSpec((seq, dv), lambda b, h, i: (b, 2 * A_HEADS + h), pipeline_mode=pl.Buffered(1)),
            pl.BlockSpec((4, HEAD_DIM), lambda b, h, i: (0, 0)),
            pl.BlockSpec((1, dv), lambda b, h, i: (0, 0)),
        ],
        out_specs=pl.BlockSpec((tq, dv), lambda b, h, i: (b * nq + i, h)),
        scratch_shapes=[pltpu.VMEM((2, tq, seq), F32), pltpu.VMEM((8, LANES), F32)],
        compiler_params=_params(("parallel", "parallel", "arbitrary")),
        name="attn_a",
    )(qkv, qkv, qkv, lam_p, subln.reshape(1, dv))


def _band_kernel(*refs, half, tl, seq_len, n_kv, group, has_sink, emit_lse):
    if has_sink:
        sink_ref, refs = refs[0], refs[1:]
    q_ref, kp_ref, kc_ref, kn_ref, vp_ref, vc_ref, vn_ref = refs[:7]
    o_ref = refs[7]
    lse_ref = refs[8] if emit_lse else None
    i = pl.program_id(1)
    win = tl + 2 * half
    qpos = i * tl + lax.broadcasted_iota(jnp.int32, (tl, win), 0)
    kpos = i * tl - half + lax.broadcasted_iota(jnp.int32, (tl, win), 1)
    dist = jnp.abs(kpos - qpos)
    dist = jnp.where(kpos < 0, win, dist)
    dist = jnp.where(kpos >= seq_len, win, dist)
    bias = jnp.where(dist <= half, 0.0, NEG)
    lane = lax.broadcasted_iota(jnp.int32, (tl, LANES), 1)
    lse_all = jnp.zeros((tl, LANES), F32)
    for kv in range(n_kv):
        cs = slice(kv * HEAD_DIM, (kv + 1) * HEAD_DIM)
        kw = jnp.concatenate([kp_ref[:, cs], kc_ref[:, cs], kn_ref[:, cs]], axis=0)
        vw = jnp.concatenate([vp_ref[:, cs], vc_ref[:, cs], vn_ref[:, cs]], axis=0)
        for g in range(group):
            hd = kv * group + g
            hs = slice(hd * HEAD_DIM, (hd + 1) * HEAD_DIM)
            s = lax.dot_general(q_ref[:, hs], kw, NT_DIMS, preferred_element_type=F32) + bias
            m = jnp.max(s, axis=1, keepdims=True)
            if has_sink:
                sk = sink_ref[hd] * LOG2E
                m = jnp.maximum(m, sk)
            e = jnp.exp2(s - m)
            den = jnp.sum(e, axis=1, keepdims=True)
            if has_sink:
                den = den + jnp.exp2(sk - m)
            o = _dot(e.astype(BF16), vw) / den
            o_ref[:, hs] = o.astype(o_ref.dtype)
            if emit_lse:
                lse_all = jnp.where(lane == hd, m + jnp.log2(den), lse_all)
    if emit_lse:
        lse_ref[...] = lse_all


def band_attn(qkv, *, q_col, k_col, v_col, n_kv, group, half, tl, sink=None, emit_lse=False):
    nb, seq_len, _ = qkv.shape
    wq = n_kv * group * HEAD_DIM
    wk = n_kv * HEAD_DIM
    nq = seq_len // tl
    hb = tl // half
    nh = seq_len // half
    kern = functools.partial(_band_kernel, half=half, tl=tl, seq_len=seq_len, n_kv=n_kv, group=group,
                             has_sink=sink is not None, emit_lse=emit_lse)

    def halo(col):
        prev = pl.BlockSpec((None, half, wk), lambda b, i: (b, jnp.maximum(i * hb - 1, 0), col))
        cur = pl.BlockSpec((None, tl, wk), lambda b, i: (b, i, col))
        nxt = pl.BlockSpec((None, half, wk), lambda b, i: (b, jnp.minimum((i + 1) * hb, nh - 1), col))
        return [prev, cur, nxt]

    in_specs = [pl.BlockSpec((None, tl, wq), lambda b, i: (b, i, q_col))] + halo(k_col) + halo(v_col)
    args = [qkv] * 7
    if sink is not None:
        in_specs = [pl.BlockSpec(memory_space=pltpu.SMEM)] + in_specs
        args = [sink] + args
    out_spec = pl.BlockSpec((None, tl, wq), lambda b, i: (b, i, 0))
    o_shape = jax.ShapeDtypeStruct((nb, seq_len, wq), BF16)
    if emit_lse:
        out_shape = (o_shape, jax.ShapeDtypeStruct((nb, seq_len, LANES), F32))
        out_specs = (out_spec, pl.BlockSpec((None, tl, LANES), lambda b, i: (b, i, 0)))
    else:
        out_shape, out_specs = o_shape, out_spec
    return pl.pallas_call(
        kern,
        out_shape=out_shape,
        grid=(nb, nq),
        in_specs=in_specs,
        out_specs=out_specs,
        compiler_params=_params(("parallel", "arbitrary")),
        name="band_attn",
    )(*args)


def _outproj_kernel(a_ref, w_ref, gate_ref, *refs, n_a):
    x_refs, o_ref = refs[:-1], refs[-1]
    o_ref[...] = _read_rows(x_refs, n_a) + gate_ref[...] * _dot(a_ref[...], w_ref[...])


def outproj(a, w, x, mod3, gate_chunk, seq, *, tm, tn):
    t, k = a.shape
    d = w.shape[1]
    tps = seq // tm
    nj = d // tn
    xs = _row_sources(x)
    return pl.pallas_call(
        functools.partial(_outproj_kernel, n_a=xs[0].shape[0] // tm),
        out_shape=jax.ShapeDtypeStruct((t, d), F32),
        grid=(t // tm, nj),
        in_specs=[
            pl.BlockSpec((tm, k), lambda i, j: (i, 0)),
            pl.BlockSpec((k, tn), lambda i, j: (0, j)),
            pl.BlockSpec((None, 1, tn), lambda i, j: (i // tps, 0, gate_chunk * nj + j)),
        ] + _row_specs(xs, tm, tn, lambda j: j),
        out_specs=pl.BlockSpec((tm, tn), lambda i, j: (i, j)),
        compiler_params=_params(("parallel", "arbitrary")),
        name="outproj",
    )(a, w, mod3, *xs)


def _merge_outproj_kernel(*refs, dils):
    n_g = len(dils)
    o_refs, l_refs = refs[:n_g], refs[n_g:2 * n_g]
    w_ref, x_ref, gate_ref, o_ref, a_scr = refs[2 * n_g:2 * n_g + 5]
    un_scr = refs[2 * n_g + 5:]
    j = pl.program_id(1)

    @pl.when(j == 0)
    def _():
        n_heads = a_scr.shape[1] // LANES
        slabs, ls = [], []
        si = 0
        for og_ref, lg_ref, dil in zip(o_refs, l_refs, dils):
            if dil == 1:
                slabs.append([og_ref[0, :, c * LANES:(c + 1) * LANES].astype(F32) for c in range(n_heads)])
                ls.append(lg_ref[0])
            else:
                so, sl = un_scr[si], un_scr[si + 1]
                si += 2
                n = og_ref.shape[1]
                for r in range(dil):
                    sl[pl.ds(r, n, stride=dil), :] = lg_ref[r]
                    for c in range(n_heads):
                        so[c, pl.ds(r, n, stride=dil), :] = og_ref[r, :, c * LANES:(c + 1) * LANES].astype(F32)
                slabs.append([so[c] for c in range(n_heads)])
                ls.append(sl[...])
        m = functools.reduce(jnp.maximum, ls)
        es = [jnp.exp2(l - m) for l in ls]
        den = functools.reduce(lambda a, b: a + b, es)
        alphas = [e / den for e in es]
        for c in range(n_heads):
            merged = functools.reduce(lambda a, b: a + b,
                                      [al[:, c:c + 1] * sg[c] for al, sg in zip(alphas, slabs)])
            a_scr[:, c * LANES:(c + 1) * LANES] = merged.astype(BF16)

    o_ref[...] = x_ref[...] + gate_ref[...] * _dot(a_scr[...], w_ref[...])


def merge_outproj(os_, lses, dils, w, x, mod3, gate_chunk, seq, *, tm, tn):
    k = os_[0].shape[-1]
    t, d = x.shape
    tps = seq // tm
    nj = d // tn
    grp = lambda dil, width: pl.BlockSpec((None, dil, tm // dil, width),
                                          lambda i, j: (i // tps, 0, i % tps, 0))
    un_scratch = []
    for dil in dils:
        if dil != 1:
            un_scratch += [pltpu.VMEM((k // LANES, tm, LANES), F32), pltpu.VMEM((tm, LANES), F32)]
    kern = functools.partial(_merge_outproj_kernel, dils=tuple(dils))
    return pl.pallas_call(
        kern,
        out_shape=jax.ShapeDtypeStruct((t, d), F32),
        grid=(t // tm, nj),
        in_specs=[grp(dil, k) for dil in dils] + [grp(dil, LANES) for dil in dils] + [
            pl.BlockSpec((k, tn), lambda i, j: (0, j)),
            pl.BlockSpec((tm, tn), lambda i, j: (i, j)),
            pl.BlockSpec((None, 1, tn), lambda i, j: (i // tps, 0, gate_chunk * nj + j)),
        ],
        out_specs=pl.BlockSpec((tm, tn), lambda i, j: (i, j)),
        scratch_shapes=[pltpu.VMEM((tm, k), BF16)] + un_scratch,
        compiler_params=_params(("parallel", "arbitrary")),
        name="merge_outproj",
    )(*os_, *lses, w, x, mod3)


def _silu_mul(g, u):
    return (g * jax.nn.sigmoid(g)) * u


def _ffn_up_kernel(x_ref, g_ref, sc_ref, sh_ref, wg_ref, wu_ref, o_ref, h_scr):
    j = pl.program_id(1)

    def project(h_rows):
        return _silu_mul(_dot(h_rows, wg_ref[...]), _dot(h_rows, wu_ref[...])).astype(BF16)

    @pl.when(j == 0)
    def _():
        g, sc, sh = g_ref[...], sc_ref[...], sh_ref[...]
        rc = x_ref.shape[0] // NORM_CHUNKS
        for c in range(NORM_CHUNKS):
            rs = slice(c * rc, (c + 1) * rc)
            hb = _norm_mod(x_ref[rs, :], g, sc, sh).astype(BF16)
            h_scr[rs, :] = hb
            o_ref[rs, :] = project(hb)

    @pl.when(j > 0)
    def _():
        o_ref[...] = project(h_scr[...])


def ffn_up(x, mod3, g, w_gu, seq, *, tm, tn):
    t, d = x.shape
    f = w_gu.shape[1] // 2
    tps = seq // tm
    nj = f // tn
    mod_spec = lambda k: pl.BlockSpec((None, 1, d), lambda i, j: (i // tps, 0, k))
    return pl.pallas_call(
        _ffn_up_kernel,
        out_shape=jax.ShapeDtypeStruct((t, f), BF16),
        grid=(t // tm, nj),
        in_specs=[
            pl.BlockSpec((tm, d), lambda i, j: (i, 0)),
            pl.BlockSpec((1, d), lambda i, j: (0, 0)),
            mod_spec(4), mod_spec(3),
            pl.BlockSpec((d, tn), lambda i, j: (0, j)),
            pl.BlockSpec((d, tn), lambda i, j: (0, nj + j)),
        ],
        out_specs=pl.BlockSpec((tm, tn), lambda i, j: (i, j)),
        scratch_shapes=[pltpu.VMEM((tm, d), BF16)],
        compiler_params=_params(("parallel", "arbitrary")),
        name="ffn_up",
    )(x, g.reshape(1, d), mod3, mod3, w_gu, w_gu)


MOE_TM = 512
R_E1, R_E2, R_W1, R_W2, R_RANK1, R_RANK2 = range(6)


def _moe_route_kernel(x_ref, g_ref, sc_ref, sh_ref, r_ref, tri_ref, route_ref, cnt_ref, cnt_scr):
    i = pl.program_id(0)

    @pl.when(i == 0)
    def _():
        cnt_scr[...] = jnp.zeros_like(cnt_scr)

    h = _norm_mod(x_ref[...], g_ref[...], sc_ref[...], sh_ref[...])
    logits = _dot3(h, r_ref[...])
    lane = lax.broadcasted_iota(jnp.int32, logits.shape, 1).astype(F32)
    logits = jnp.where(lane < N_EXPERTS, logits, NEG)
    v1 = jnp.max(logits, axis=1, keepdims=True)
    i1 = jnp.min(jnp.where(logits == v1, lane, float(LANES)), axis=1, keepdims=True)
    rest = jnp.where(lane == i1, NEG, logits)
    v2 = jnp.max(rest, axis=1, keepdims=True)
    i2 = jnp.min(jnp.where(rest == v2, lane, float(LANES)), axis=1, keepdims=True)
    e2 = jnp.exp(v2 - v1)
    w1 = 1.0 / (1.0 + e2)
    w2 = e2 / (1.0 + e2)
    sel = jnp.where(lane == i1, 1.0, 0.0) + jnp.where(lane == i2, 1.0, 0.0)
    base = cnt_scr[0:1, :]
    rank = _dot(tri_ref[...], sel.astype(BF16)) + base
    r1 = jnp.sum(jnp.where(lane == i1, rank, 0.0), axis=1, keepdims=True)
    r2 = jnp.sum(jnp.where(lane == i2, rank, 0.0), axis=1, keepdims=True)
    slab = jnp.zeros_like(logits)
    for col, val in ((R_E1, i1), (R_E2, i2), (R_W1, w1), (R_W2, w2), (R_RANK1, r1), (R_RANK2, r2)):
        slab = jnp.where(lane == col, val, slab)
    route_ref[...] = slab
    total = base + jnp.sum(sel, axis=0, keepdims=True)
    cnt_scr[0:1, :] = total
    cnt_ref[...] = jnp.broadcast_to(total, cnt_ref.shape)


def moe_route(x, mod3, g, router, seq, *, tm):
    t, d = x.shape
    tps = seq // tm
    mod_spec = lambda k: pl.BlockSpec((None, 1, d), lambda i: (i // tps, 0, k))
    router_p = jnp.pad(router, ((0, 0), (0, LANES - router.shape[1])))
    tri = (lax.broadcasted_iota(jnp.int32, (tm, tm), 1)
           < lax.broadcasted_iota(jnp.int32, (tm, tm), 0)).astype(BF16)
    return pl.pallas_call(
        _moe_route_kernel,
        out_shape=(jax.ShapeDtypeStruct((t, LANES), F32), jax.ShapeDtypeStruct((8, LANES), F32)),
        grid=(t // tm,),
        in_specs=[
            pl.BlockSpec((tm, d), lambda i: (i, 0)),
            pl.BlockSpec((1, d), lambda i: (0, 0)),
            mod_spec(4), mod_spec(3),
            pl.BlockSpec((d, LANES), lambda i: (0, 0)),
            pl.BlockSpec((tm, tm), lambda i: (0, 0)),
        ],
        out_specs=(pl.BlockSpec((tm, LANES), lambda i: (i, 0)), pl.BlockSpec((8, LANES), lambda i: (0, 0))),
        scratch_shapes=[pltpu.VMEM((8, LANES), F32)],
        compiler_params=_params(("arbitrary",)),
        name="moe_route",
    )(x, g.reshape(1, d), mod3, mod3, router_p, tri)


def _moe_plan(route, counts, n_tok, tm):
    n_tiles_max = 2 * n_tok // MOE_TM + N_EXPERTS
    cnt = counts[0, :N_EXPERTS].astype(jnp.int32)
    tiles = (cnt + MOE_TM - 1) // MOE_TM
    tile_end = jnp.cumsum(tiles)
    tile_start = tile_end - tiles
    row_start = tile_start * MOE_TM
    n_tiles = tile_end[-1]
    e1 = route[:, R_E1].astype(jnp.int32)
    e2 = route[:, R_E2].astype(jnp.int32)
    pos1 = row_start[e1] + route[:, R_RANK1].astype(jnp.int32)
    pos2 = row_start[e2] + route[:, R_RANK2].astype(jnp.int32)
    shape = (n_tok // tm, 1, tm)
    tile_id = jnp.arange(n_tiles_max, dtype=jnp.int32)
    tile_expert = jnp.minimum(jnp.sum((tile_id[:, None] >= tile_end[None, :]).astype(jnp.int32), axis=1),
                              N_EXPERTS - 1)
    last_tile = jnp.where(tiles > 0, tile_end - 1, -1)
    tail_tile = n_tiles + jnp.arange(N_EXPERTS, dtype=jnp.int32)
    tail_tile = jnp.where(tail_tile < n_tiles_max, tail_tile, -1)
    zero_tiles = jnp.concatenate([last_tile, tail_tile])
    zero_rows = jnp.where(zero_tiles >= 0, zero_tiles * MOE_TM, -1).astype(jnp.int32)
    return pos1.reshape(shape), pos2.reshape(shape), tile_expert.astype(jnp.int32), zero_rows, n_tiles_max


def _moe_dispatch_kernel(zero_ref, pos1_ref, pos2_ref, x_ref, g_ref, sc_ref, sh_ref, xs_ref,
                         h_scr, zero_scr, sem, zsem):
    i = pl.program_id(0)
    tm = x_ref.shape[0]

    @pl.when(i == 0)
    def _():
        zero_scr[...] = jnp.zeros_like(zero_scr)
        for z in range(2 * N_EXPERTS):
            @pl.when(zero_ref[z] >= 0)
            def _():
                row = pl.multiple_of(zero_ref[z], MOE_TM)
                pltpu.make_async_copy(zero_scr, xs_ref.at[pl.ds(row, MOE_TM), :], zsem).start()
        for z in range(2 * N_EXPERTS):
            @pl.when(zero_ref[z] >= 0)
            def _():
                pltpu.make_async_copy(zero_scr, xs_ref.at[pl.ds(0, MOE_TM), :], zsem).wait()

    h_scr[...] = _norm_mod(x_ref[...], g_ref[...], sc_ref[...], sh_ref[...])

    def issue(r, carry):
        src = h_scr.at[pl.ds(r, 1), :]
        pltpu.make_async_copy(src, xs_ref.at[pl.ds(pos1_ref[0, r], 1), :], sem).start()
        pltpu.make_async_copy(src, xs_ref.at[pl.ds(pos2_ref[0, r], 1), :], sem).start()
        return carry

    lax.fori_loop(0, tm, issue, 0, unroll=8)
    for _ in range(2):
        pltpu.make_async_copy(h_scr, xs_ref.at[pl.ds(0, tm), :], sem).wait()


def moe_dispatch(x, mod3, g, pos1, pos2, zero_rows, n_rows, seq, *, tm):
    t, d = x.shape
    tps = seq // tm
    mod_spec = lambda k: pl.BlockSpec((None, 1, d), lambda i, *_: (i // tps, 0, k))
    pos_spec = pl.BlockSpec((None, 1, tm), lambda i, *_: (i, 0, 0), memory_space=pltpu.SMEM)
    return pl.pallas_call(
        _moe_dispatch_kernel,
        out_shape=jax.ShapeDtypeStruct((n_rows, d), F32),
        grid_spec=pltpu.PrefetchScalarGridSpec(
            num_scalar_prefetch=1,
            grid=(t // tm,),
            in_specs=[
                pos_spec, pos_spec,
                pl.BlockSpec((tm, d), lambda i, *_: (i, 0)),
                pl.BlockSpec((1, d), lambda i, *_: (0, 0)),
                mod_spec(4), mod_spec(3),
            ],
            out_specs=pl.BlockSpec(memory_space=pl.ANY),
            scratch_shapes=[pltpu.VMEM((tm, d), F32), pltpu.VMEM((MOE_TM, d), F32),
                            pltpu.SemaphoreType.DMA, pltpu.SemaphoreType.DMA],
        ),
        compiler_params=_params(("arbitrary",)),
        name="moe_dispatch",
    )(zero_rows, pos1, pos2, x, g.reshape(1, d), mod3, mod3)


def _moe_expert_kernel(expert_ref, xs_ref, wgu_ref, wd_ref, ys_ref):
    del expert_ref
    f = wd_ref.shape[0]
    gu = _dot(xs_ref[...].astype(BF16), wgu_ref[...])
    act = _silu_mul(gu[:, :f], gu[:, f:]).astype(BF16)
    ys_ref[...] = _dot(act, wd_ref[...])


def moe_expert(xs, w_gu, w_down, tile_expert):
    n_rows, d = xs.shape
    _, _, two_f = w_gu.shape
    return pl.pallas_call(
        _moe_expert_kernel,
        out_shape=jax.ShapeDtypeStruct((n_rows, d), F32),
        grid_spec=pltpu.PrefetchScalarGridSpec(
            num_scalar_prefetch=1,
            grid=(n_rows // MOE_TM,),
            in_specs=[
                pl.BlockSpec((MOE_TM, d), lambda s, ex: (s, 0)),
                pl.BlockSpec((None, d, two_f), lambda s, ex: (ex[s], 0, 0)),
                pl.BlockSpec((None, two_f // 2, d), lambda s, ex: (ex[s], 0, 0)),
            ],
            out_specs=pl.BlockSpec((MOE_TM, d), lambda s, ex: (s, 0)),
        ),
        compiler_params=_params(("arbitrary",)),
        name="moe_expert",
    )(tile_expert, xs, w_gu, w_down)


def _moe_combine_kernel(pos1_ref, pos2_ref, x_ref, route_ref, gate_ref, ys_ref, *refs, n_a):
    if n_a is None:
        o_ref, y1_scr, y2_scr, sem = refs
    else:
        fg_ref, oa_ref, ob_ref, y1_scr, y2_scr, sem = refs
    tm = x_ref.shape[0]

    def issue(r, carry):
        pltpu.make_async_copy(ys_ref.at[pl.ds(pos1_ref[0, r], 1), :], y1_scr.at[pl.ds(r, 1), :], sem).start()
        pltpu.make_async_copy(ys_ref.at[pl.ds(pos2_ref[0, r], 1), :], y2_scr.at[pl.ds(r, 1), :], sem).start()
        return carry

    lax.fori_loop(0, tm, issue, 0, unroll=8)
    for buf in (y1_scr, y2_scr):
        pltpu.make_async_copy(ys_ref.at[pl.ds(0, tm), :], buf, sem).wait()
    route = route_ref[...]
    w1 = route[:, R_W1:R_W1 + 1]
    w2 = route[:, R_W2:R_W2 + 1]
    x = x_ref[...] + gate_ref[...] * (w1 * y1_scr[...] + w2 * y2_scr[...])
    if n_a is None:
        o_ref[...] = x
        return
    ms = jnp.mean(x * x, axis=-1, keepdims=True)
    y = (x * lax.rsqrt(ms + NORM_EPS)) * fg_ref[...]
    i = pl.program_id(0)

    @pl.when(i < n_a)
    def _():
        oa_ref[...] = y

    @pl.when(i >= n_a)
    def _():
        ob_ref[...] = y


def moe_combine(x, mod3, route, pos1, pos2, ys, seq, *, tm, final=None):
    t, d = x.shape
    tps = seq // tm
    pos_spec = pl.BlockSpec((None, 1, tm), lambda i: (i, 0, 0), memory_space=pltpu.SMEM)
    in_specs = [
        pos_spec, pos_spec,
        pl.BlockSpec((tm, d), lambda i: (i, 0)),
        pl.BlockSpec((tm, LANES), lambda i: (i, 0)),
        pl.BlockSpec((None, 1, d), lambda i: (i // tps, 0, 5)),
        pl.BlockSpec(memory_space=pl.ANY),
    ]
    args = [pos1, pos2, x, route, mod3, ys]
    if final is None:
        n_a = None
        out_shape = jax.ShapeDtypeStruct((t, d), F32)
        out_specs = pl.BlockSpec((tm, d), lambda i: (i, 0))
    else:
        final_g, rows_a = final
        n_a = rows_a // tm
        in_specs.append(pl.BlockSpec((1, d), lambda i: (0, 0)))
        args.append(final_g.reshape(1, d))
        out_shape = (jax.ShapeDtypeStruct((rows_a, d), F32), jax.ShapeDtypeStruct((t - rows_a, d), F32))
        out_specs = (pl.BlockSpec((tm, d), lambda i: (jnp.minimum(i, n_a - 1), 0)),
                     pl.BlockSpec((tm, d), lambda i: (jnp.maximum(i - n_a, 0), 0)))
    return pl.pallas_call(
        functools.partial(_moe_combine_kernel, n_a=n_a),
        out_shape=out_shape,
        grid=(t // tm,),
        in_specs=in_specs,
        out_specs=out_specs,
        scratch_shapes=[pltpu.VMEM((tm, d), F32), pltpu.VMEM((tm, d), F32), pltpu.SemaphoreType.DMA],
        compiler_params=_params(("arbitrary",)),
        name="moe_combine",
    )(*args)


def moe_ffn(x, mod3, g, router, w_gu, w_down, seq, *, tm, final=None):
    t = x.shape[0]
    route, counts = moe_route(x, mod3, g, router, seq, tm=tm)
    pos1, pos2, tile_expert, zero_rows, n_tiles_max = _moe_plan(route, counts, t, tm)
    xs = moe_dispatch(x, mod3, g, pos1, pos2, zero_rows, n_tiles_max * MOE_TM, seq, tm=tm)
    ys = moe_expert(xs, w_gu, w_down, tile_expert)
    return moe_combine(x, mod3, route, pos1, pos2, ys, seq, tm=tm, final=final)


def _lambda_init(layer):
    return 0.8 - 0.6 * math.exp(-0.3 * layer)


def _mixer_a(x, mod3, g, w_in, w_out, lam_p, subln, tables, nb, seq, layer, cfg):
    tm_in = cfg["tm"] if len(_row_sources(x)) > 1 else cfg["tm_in"]
    qkv = inproj(x, mod3, g, w_in, tables, seq, q_cols=D_MODEL, rope_cols=2 * D_MODEL,
                 tm=tm_in, tn=cfg["tn_in"]).reshape(nb * seq, 3 * D_MODEL)
    o = attn_a(qkv, lam_p, subln, nb, seq, _lambda_init(layer), tq=cfg["tq_a"], tk=cfg["tk_a"],
               tk_pv=cfg["tk_pv_a"], unroll=cfg["unroll_a"])
    return outproj(o, w_out, x, mod3, 2, seq, tm=cfg["tm"], tn=cfg["tn_out_mix"])


def _mixer_b(x, mod3, g, w_in, w_out, sink, tables, nb, seq, cfg):
    nq = B_HEADS * HEAD_DIM
    qkv = inproj(x, mod3, g, w_in, tables, seq, q_cols=nq, rope_cols=nq + B_KV_HEADS * HEAD_DIM,
                 tm=cfg["tm_in"], tn=cfg["tn_in_b"]).reshape(nb, seq, -1)
    o = band_attn(qkv, q_col=0, k_col=4, v_col=5, n_kv=B_KV_HEADS, group=B_HEADS // B_KV_HEADS,
                  half=B_WINDOW, tl=cfg["tl_b"], sink=sink)
    return outproj(o.reshape(nb * seq, nq), w_out, x, mod3, 2, seq, tm=cfg["tm"], tn=cfg["tn_out_mix"])


def _mixer_c(x, mod3, g, w_in, w_out, tables, nb, seq, cfg):
    n_groups = len(C_CONFIGS)
    os_, lses, dils = [], [], []
    for gi, (window, dil) in enumerate(C_CONFIGS):
        qkv = inproj(x, mod3, g, w_in, tables, seq, q_cols=C_WIDTH, rope_cols=2 * C_WIDTH,
                     tm=cfg["tm_in"], tn=C_WIDTH, dil=dil, col_stride=n_groups, col_off=gi,
                     n_cols=3 * C_WIDTH)
        sub = seq // dil
        o, lse = band_attn(qkv.reshape(nb * dil, sub, 3 * C_WIDTH), q_col=0, k_col=1, v_col=2,
                           n_kv=C_HEADS, group=1, half=window // (2 * dil), tl=cfg["tl_c"],
                           emit_lse=True)
        os_.append(o.reshape(nb, dil, sub, C_WIDTH))
        lses.append(lse.reshape(nb, dil, sub, LANES))
        dils.append(dil)
    return merge_outproj(os_, lses, dils, w_out, x, mod3, 2, seq, tm=cfg["tm_merge"],
                         tn=cfg["tn_out_ffn"])


def _default_cfg(seq):
    return dict(tm=min(512, seq), tm_in=min(1024, seq), tm_up=min(1024, seq), tn_in=1024, tn_in_b=512,
                tn_out_mix=D_MODEL, tn_out_ffn=1024,
                tq_a=min(512, seq), tk_a=min(512, seq), tl_b=min(256, seq),
                tl_c=min(128, seq // C_CONFIGS[-1][1]), tm_merge=min(512, seq), tn_up=512,
                unroll_a=True, tk_pv_a=min(2048, seq))


def trunk_all(x, c8, nb, seq, ada_w, ada_b, norm_mix, norm_ffn, a_w_in, a_w_out, a_lambda, a_subln,
              b_w_in, b_w_out, b_sink, c_w_in, c_w_out, f_w_gu, f_w_down, moe_router, moe_w_gu,
              moe_w_down, final_norm_g, cfg=None):
    assert (DEPTH - 1) % 2 == 1, "the final norm is fused into the last layer's routed-FFN combine"
    rows_a = x[0].shape[0]
    cfg = cfg or _default_cfg(seq)
    bf = lambda w: w.astype(BF16)
    mod_all = adaln(c8, ada_w, ada_b)
    tables = _rope_tables(seq)
    for i in range(DEPTH):
        mod3 = mod_all[i].reshape(8, 1, 6 * D_MODEL)
        kind, jm = i % 3, i // 3
        if kind == 0:
            x = _mixer_a(x, mod3, norm_mix[i], bf(a_w_in[jm]), bf(a_w_out[jm]), a_lambda[jm],
                         a_subln[jm], tables, nb, seq, i, cfg)
        elif kind == 1:
            x = _mixer_b(x, mod3, norm_mix[i], bf(b_w_in[jm]), bf(b_w_out[jm]), b_sink[jm], tables,
                         nb, seq, cfg)
        else:
            x = _mixer_c(x, mod3, norm_mix[i], bf(c_w_in[jm]), bf(c_w_out[jm]), tables, nb, seq, cfg)
        jf = i // 2
        if i % 2 == 0:
            act = ffn_up(x, mod3, norm_ffn[i], bf(f_w_gu[jf]), seq, tm=cfg["tm_up"], tn=cfg["tn_up"])
            x = outproj(act, bf(f_w_down[jf]), x, mod3, 5, seq, tm=cfg["tm"], tn=cfg["tn_out_ffn"])
        else:
            x = moe_ffn(x, mod3, norm_ffn[i], moe_router[jf], bf(moe_w_gu[jf]), bf(moe_w_down[jf]), seq,
                        tm=cfg["tm"], final=(final_norm_g, rows_a) if i == DEPTH - 1 else None)
    return x


def kernel(x_prompt, x_sample, c_prompt, c_sample, ada_w, ada_b, norm_mix, norm_ffn, a_w_in, a_w_out,
           a_lambda, a_subln, b_w_in, b_w_out, b_sink, c_w_in, c_w_out, f_w_gu, f_w_down, moe_router,
           moe_w_gu, moe_w_down, final_norm):
    bp, seq, d = x_prompt.shape
    bs = x_sample.shape[0]
    assert x_sample.shape[1] == seq
    nb = bp + bs
    x = (x_prompt.reshape(bp * seq, d), x_sample.reshape(bs * seq, d))
    c8 = jnp.concatenate([c_prompt, c_sample, jnp.zeros((8 - nb, d), F32)], axis=0)
    ya, yb = trunk_all(x, c8, nb, seq, ada_w, ada_b, norm_mix, norm_ffn, a_w_in, a_w_out, a_lambda,
                       a_subln, b_w_in, b_w_out, b_sink, c_w_in, c_w_out, f_w_gu, f_w_down, moe_router,
                       moe_w_gu, moe_w_down, final_norm)
    return (ya.reshape(bp, seq, d), yb.reshape(bs, seq, d))
```

```python
import functools
import math

import jax
import jax.numpy as jnp
from jax import lax
from jax.experimental import pallas as pl
from jax.experimental.pallas import tpu as pltpu

F32 = jnp.float32
BF16 = jnp.bfloat16

D_MODEL = 2048
DEPTH = 4
HEAD_DIM = 128
ROPE_THETA = 500000.0
ROT = HEAD_DIM // 4
NORM_EPS = 1e-6
NEG = -1e30
A_HEADS = 8
B_HEADS = 16
B_KV_HEADS = 4
B_WINDOW = 128
C_CONFIGS = ((128, 1), (512, 4), (2048, 16))
C_HEADS = 8
C_WIDTH = C_HEADS * HEAD_DIM
FFN_DIM = 2 * D_MODEL
MOE_DIM = D_MODEL // 2
N_EXPERTS = 8

LOG2E = math.log2(math.e)
Q_SCALE = HEAD_DIM ** -0.5 * LOG2E

LANES = 128
VMEM_LIMIT = 56 * 1024 * 1024
NORM_CHUNKS = 4

NT_DIMS = (((1,), (1,)), ((), ()))


def _params(sem, vmem=VMEM_LIMIT):
    return pltpu.CompilerParams(dimension_semantics=sem, vmem_limit_bytes=vmem)


def _dot(a, b):
    return jnp.dot(a, b, preferred_element_type=F32)


def _split_bf16(x):
    hi = x.astype(BF16)
    lo = (x - hi.astype(F32)).astype(BF16)
    return hi, lo


def _dot3(a, b):
    a_hi, a_lo = _split_bf16(a)
    b_hi, b_lo = _split_bf16(b)
    return _dot(a_hi, b_hi) + (_dot(a_hi, b_lo) + _dot(a_lo, b_hi))


def _row_sources(x):
    return tuple(x) if isinstance(x, (tuple, list)) else (x,)


def _row_specs(xs, tm, width, col_of):
    if len(xs) == 1:
        return [pl.BlockSpec((tm, width), lambda i, *a: (i, col_of(*a)))]
    n_a = xs[0].shape[0] // tm
    return [pl.BlockSpec((tm, width), lambda i, *a: (jnp.minimum(i, n_a - 1), col_of(*a))),
            pl.BlockSpec((tm, width), lambda i, *a: (jnp.maximum(i - n_a, 0), col_of(*a)))]


def _read_rows(x_refs, n_a, rs=slice(None)):
    if len(x_refs) == 1:
        return x_refs[0][rs, :]
    return jnp.where(pl.program_id(0) < n_a, x_refs[0][rs, :], x_refs[1][rs, :])


def _norm_mod(x, g, sc, sh):
    ms = jnp.mean(x * x, axis=-1, keepdims=True)
    y = x * lax.rsqrt(ms + NORM_EPS)
    return (y * g) * (1.0 + sc) + sh


def _adaln_kernel(c_ref, w_ref, b_ref, o_ref):
    c = c_ref[...]
    cs = c * jax.nn.sigmoid(c)
    o_ref[...] = _dot3(cs, w_ref[...]) + b_ref[...]


def adaln(c8, ada_w, ada_b, tn=512):
    depth, d, n = ada_w.shape
    return pl.pallas_call(
        _adaln_kernel,
        out_shape=jax.ShapeDtypeStruct((depth, 8, n), F32),
        grid=(depth, n // tn),
        in_specs=[
            pl.BlockSpec((8, d), lambda l, j: (0, 0)),
            pl.BlockSpec((None, d, tn), lambda l, j: (l, 0, j)),
            pl.BlockSpec((None, 1, tn), lambda l, j: (l, 0, j)),
        ],
        out_specs=pl.BlockSpec((None, 8, tn), lambda l, j: (l, 0, j)),
        compiler_params=_params(("arbitrary", "arbitrary")),
        name="adaln",
    )(c8, ada_w, ada_b.reshape(depth, 1, n))


def _rope_tables(seq):
    pos = jnp.arange(seq, dtype=F32)
    half = ROT // 2
    inv = ROPE_THETA ** (-jnp.arange(half, dtype=F32) * (2.0 / ROT))
    ang = pos[:, None] * inv[None, :]
    cos, sin = jnp.cos(ang), jnp.sin(ang)
    zeros = lambda n: jnp.zeros((seq, n), F32)
    cos_t = jnp.concatenate([cos, cos, jnp.ones((seq, HEAD_DIM - ROT), F32)], axis=1)
    sin_a = jnp.concatenate([zeros(half), sin, zeros(HEAD_DIM - ROT)], axis=1)
    sin_b = jnp.concatenate([-sin, zeros(HEAD_DIM - half)], axis=1)
    return cos_t, sin_a, sin_b


def _perm_tables(tables, dil):
    return tuple(t.reshape(-1, dil, HEAD_DIM).transpose(1, 0, 2) for t in tables)


def _inproj_kernel(*refs, n_x, n_a, n_q, n_rope, tn, dil):
    x_refs = refs[:n_x]
    g_ref, sc_ref, sh_ref, w_ref, cos_ref, sa_ref, sb_ref, o_ref, h_scr = refs[n_x:n_x + 9]
    maybe_slab_scr = refs[n_x + 9:]
    j = pl.program_id(1)
    tm, d = x_refs[0].shape
    n = tm // dil
    assert n_q >= 1

    def project(h_rows, rr, ll, scale):
        shape = (rr.stop - rr.start, ll.stop - ll.start, HEAD_DIM)
        cos, sa, sb = (ref[rr, ll, :].reshape(-1, HEAD_DIM) * scale for ref in (cos_ref, sa_ref, sb_ref))
        acc = _dot(h_rows, w_ref[...])
        for hh in range(tn // HEAD_DIM):
            cols = slice(hh * HEAD_DIM, (hh + 1) * HEAD_DIM)
            a = acc[:, cols]
            r = a * cos + pltpu.roll(a, ROT // 2, 1) * sa + pltpu.roll(a, HEAD_DIM - ROT // 2, 1) * sb
            o_ref[rr, ll, cols] = r.reshape(shape).astype(BF16)

    whole = (slice(0, dil), slice(0, n))

    @pl.when(j == 0)
    def _():
        g, sc, sh = g_ref[...], sc_ref[...], sh_ref[...]
        if dil == 1:
            rc = tm // NORM_CHUNKS
            for c in range(NORM_CHUNKS):
                rs = slice(c * rc, (c + 1) * rc)
                hb = _norm_mod(_read_rows(x_refs, n_a, rs), g, sc, sh).astype(BF16)
                h_scr[rs, :] = hb
                project(hb, slice(0, 1), rs, Q_SCALE)
        else:
            h = _norm_mod(_read_rows(x_refs, n_a), g, sc, sh)
            (slab_scr,) = maybe_slab_scr
            for c in range(d // LANES):
                slab_scr[c] = h[:, c * LANES:(c + 1) * LANES]
            per = dil // NORM_CHUNKS
            for k in range(NORM_CHUNKS):
                for r in range(k * per, (k + 1) * per):
                    for c in range(d // LANES):
                        rows = slab_scr[c, pl.ds(r, n, stride=dil), :]
                        h_scr[r * n:(r + 1) * n, c * LANES:(c + 1) * LANES] = rows.astype(BF16)
                project(h_scr[k * per * n:(k + 1) * per * n, :], slice(k * per, (k + 1) * per),
                        slice(0, n), Q_SCALE)

    @pl.when(jnp.logical_and(j > 0, j < n_rope))
    def _():
        project(h_scr[...], *whole, jnp.where(j < n_q, Q_SCALE, 1.0).astype(F32))

    @pl.when(j >= n_rope)
    def _():
        o_ref[...] = _dot(h_scr[...], w_ref[...]).reshape(dil, n, tn).astype(BF16)


def inproj(x, mod3, g, w, tables, seq, *, q_cols, rope_cols, tm, tn, dil=1, col_stride=1, col_off=0,
           n_cols=None):
    xs = _row_sources(x)
    t = sum(a.shape[0] for a in xs)
    d = xs[0].shape[1]
    n_cols = n_cols or w.shape[1]
    tps = seq // tm
    n = tm // dil
    kern = functools.partial(_inproj_kernel, n_x=len(xs), n_a=xs[0].shape[0] // tm, n_q=q_cols // tn,
                             n_rope=rope_cols // tn, tn=tn, dil=dil)
    mod_spec = lambda k: pl.BlockSpec((None, 1, d), lambda i, j: (i // tps, 0, k))
    tab_spec = pl.BlockSpec((dil, n, HEAD_DIM), lambda i, j: (0, i % tps, 0))
    return pl.pallas_call(
        kern,
        out_shape=jax.ShapeDtypeStruct((t // seq, dil, seq // dil, n_cols), BF16),
        grid=(t // tm, n_cols // tn),
        in_specs=_row_specs(xs, tm, d, lambda j: 0) + [
            pl.BlockSpec((1, d), lambda i, j: (0, 0)),
            mod_spec(1), mod_spec(0),
            pl.BlockSpec((d, tn), lambda i, j: (0, j * col_stride + col_off)),
            tab_spec, tab_spec, tab_spec,
        ],
        out_specs=pl.BlockSpec((None, dil, n, tn), lambda i, j: (i // tps, 0, i % tps, j)),
        scratch_shapes=[pltpu.VMEM((tm, d), BF16)]
        + ([pltpu.VMEM((d // LANES, tm, LANES), F32)] if dil > 1 else []),
        compiler_params=_params(("parallel", "arbitrary")),
        name="inproj",
    )(*xs, g.reshape(1, d), mod3, mod3, w, *_perm_tables(tables, dil))


L_SAFE = 2.0 ** -64


def _attn_a_kernel(q_ref, k_ref, v_ref, lam_ref, subln_ref, o_ref, p_scr, knorm_scr,
                   *, tk, tk_pv, lam0, unroll):
    i = pl.program_id(2)
    seq = k_ref.shape[0]
    tq = q_ref.shape[0]
    dv = v_ref.shape[1]
    n_chunks = seq // tk

    @pl.when(i == 0)
    def _():
        def body(c, carry):
            off = pl.multiple_of(c * tk, tk)
            kf = k_ref[pl.ds(off, tk), :].astype(F32)
            sq = kf * kf
            n0 = jnp.max(jnp.sum(sq[:, :HEAD_DIM], axis=1, keepdims=True), axis=0, keepdims=True)
            n1 = jnp.max(jnp.sum(sq[:, HEAD_DIM:], axis=1, keepdims=True), axis=0, keepdims=True)
            return jnp.maximum(carry[0], n0), jnp.maximum(carry[1], n1)

        zero = jnp.zeros((1, 1), F32)
        n0, n1 = lax.fori_loop(0, n_chunks, body, (zero, zero))
        knorm_scr[0:1, :] = jnp.broadcast_to(jnp.sqrt(n0), (1, LANES))
        knorm_scr[1:2, :] = jnp.broadcast_to(jnp.sqrt(n1), (1, LANES))

    q = q_ref[...]
    qs = (q[:, :HEAD_DIM], q[:, HEAD_DIM:])
    lp = lam_ref[...].astype(F32)
    lam = (jnp.exp(jnp.sum(lp[0:1] * lp[1:2], axis=1, keepdims=True))
           - jnp.exp(jnp.sum(lp[2:3] * lp[3:4], axis=1, keepdims=True)) + lam0)

    def finish(o):
        ms = jnp.mean(o * o, axis=1, keepdims=True)
        y = (o * lax.rsqrt(ms + NORM_EPS)) * subln_ref[...]
        o_ref[...] = (y * (1.0 - lam0)).astype(BF16)

    ls = []
    for mp in range(2):
        qc = qs[mp]
        qf = qc.astype(F32)
        qn = jnp.sqrt(jnp.sum(qf * qf, axis=1, keepdims=True))
        shift = qn * (knorm_scr[mp:mp + 1, :] * 1.001) + 1.0
        cols = slice(mp * HEAD_DIM, (mp + 1) * HEAD_DIM)

        def chunk(c, lpart, mp=mp, qc=qc, shift=shift, cols=cols):
            off = pl.multiple_of(c * tk, tk)
            s = lax.dot_general(qc, k_ref[pl.ds(off, tk), cols], NT_DIMS, preferred_element_type=F32)
            for g in range(tk // LANES):
                p = jnp.exp2(s[:, g * LANES:(g + 1) * LANES] - shift)
                lpart = lpart + p
                p_scr[mp, :, pl.ds(pl.multiple_of(off + g * LANES, LANES), LANES)] = p
            return lpart

        lpart = lax.fori_loop(0, n_chunks, chunk, jnp.zeros((tq, LANES), F32), unroll=unroll)
        ls.append(jnp.sum(lpart, axis=1, keepdims=True))

    l_min = jnp.min(jnp.minimum(ls[0], ls[1]))

    @pl.when(l_min >= L_SAFE)
    def _():
        ratio = jnp.broadcast_to(lam * ls[0] / ls[1], (tq, LANES))
        acc = jnp.zeros((tq, dv), F32)
        for c in range(seq // tk_pv):
            groups = []
            for g in range(tk_pv // LANES):
                cs = slice(c * tk_pv + g * LANES, c * tk_pv + (g + 1) * LANES)
                groups.append((p_scr[0, :, cs] - ratio * p_scr[1, :, cs]).astype(BF16))
            a = jnp.concatenate(groups, axis=1)
            acc = acc + _dot(a, v_ref[c * tk_pv:(c + 1) * tk_pv, :])
        finish(acc / ls[0])

    @pl.when(l_min < L_SAFE)
    def _():
        def update(qc, kc, v, m, l, acc):
            s = lax.dot_general(qc, kc, NT_DIMS, preferred_element_type=F32)
            m_new = jnp.maximum(m, jnp.max(s, axis=1, keepdims=True))
            alpha = jnp.exp2(m - m_new)
            p = jnp.exp2(s - m_new)
            l = alpha * l + jnp.sum(p, axis=1, keepdims=True)
            acc = alpha * acc + _dot(p.astype(BF16), v)
            return m_new, l, acc

        def body(c, carry):
            m0, l0, a0, m1, l1, a1 = carry
            off = pl.multiple_of(c * tk, tk)
            k = k_ref[pl.ds(off, tk), :]
            v = v_ref[pl.ds(off, tk), :]
            m0, l0, a0 = update(qs[0], k[:, :HEAD_DIM], v, m0, l0, a0)
            m1, l1, a1 = update(qs[1], k[:, HEAD_DIM:], v, m1, l1, a1)
            return m0, l0, a0, m1, l1, a1

        m_init = jnp.full((tq, 1), NEG, F32)
        l_init = jnp.zeros((tq, 1), F32)
        a_init = jnp.zeros((tq, dv), F32)
        _, l0, a0, _, l1, a1 = lax.fori_loop(
            0, n_chunks, body, (m_init, l_init, a_init, m_init, l_init, a_init))
        finish(a0 / l0 - lam * (a1 / l1))


def attn_a(qkv, lam_p, subln, nb, seq, lam0, *, tq, tk, tk_pv, unroll):
    t = qkv.shape[0]
    dv = 2 * HEAD_DIM
    nq = seq // tq
    kern = functools.partial(_attn_a_kernel, tk=tk, tk_pv=tk_pv, lam0=lam0, unroll=unroll)
    return pl.pallas_call(
        kern,
        out_shape=jax.ShapeDtypeStruct((t, A_HEADS * dv), BF16),
        grid=(nb, A_HEADS, nq),
        in_specs=[
            pl.BlockSpec((tq, dv), lambda b, h, i: (b * nq + i, h)),
            pl.BlockSpec((seq, dv), lambda b, h, i: (b, A_HEADS + h), pipeline_mode=pl.Buffered(1)),
            pl.BlockSpec((seq, dv), lambda b, h, i: (b, 2 * A_HEADS + h), pipeline_mode=pl.Buffered(1)),
            pl.BlockSpec((4, HEAD_DIM), lambda b, h, i: (0, 0)),
            pl.BlockSpec((1, dv), lambda b, h, i: (0, 0)),
        ],
        out_specs=pl.BlockSpec((tq, dv), lambda b, h, i: (b * nq + i, h)),
        scratch_shapes=[pltpu.VMEM((2, tq, seq), F32), pltpu.VMEM((8, LANES), F32)],
        compiler_params=_params(("parallel", "parallel", "arbitrary")),
        name="attn_a",
    )(qkv, qkv, qkv, lam_p, subln.reshape(1, dv))


def _band_kernel(*refs, half, tl, seq_len, n_kv, group, has_sink, emit_lse):
    if has_sink:
        sink_ref, refs = refs[0], refs[1:]
    q_ref, kp_ref, kc_ref, kn_ref, vp_ref, vc_ref, vn_ref = refs[:7]
    o_ref = refs[7]
    lse_ref = refs[8] if emit_lse else None
    i = pl.program_id(1)
    win = tl + 2 * half
    qpos = i * tl + lax.broadcasted_iota(jnp.int32, (tl, win), 0)
    kpos = i * tl - half + lax.broadcasted_iota(jnp.int32, (tl, win), 1)
    dist = jnp.abs(kpos - qpos)
    dist = jnp.where(kpos < 0, win, dist)
    dist = jnp.where(kpos >= seq_len, win, dist)
    bias = jnp.where(dist <= half, 0.0, NEG)
    lane = lax.broadcasted_iota(jnp.int32, (tl, LANES), 1)
    lse_all = jnp.zeros((tl, LANES), F32)
    for kv in range(n_kv):
        cs = slice(kv * HEAD_DIM, (kv + 1) * HEAD_DIM)
        kw = jnp.concatenate([kp_ref[:, cs], kc_ref[:, cs], kn_ref[:, cs]], axis=0)
        vw = jnp.concatenate([vp_ref[:, cs], vc_ref[:, cs], vn_ref[:, cs]], axis=0)
        for g in range(group):
            hd = kv * group + g
            hs = slice(hd * HEAD_DIM, (hd + 1) * HEAD_DIM)
            s = lax.dot_general(q_ref[:, hs], kw, NT_DIMS, preferred_element_type=F32) + bias
            m = jnp.max(s, axis=1, keepdims=True)
            if has_sink:
                sk = sink_ref[hd] * LOG2E
                m = jnp.maximum(m, sk)
            e = jnp.exp2(s - m)
            den = jnp.sum(e, axis=1, keepdims=True)
            if has_sink:
                den = den + jnp.exp2(sk - m)
            o = _dot(e.astype(BF16), vw) / den
            o_ref[:, hs] = o.astype(o_ref.dtype)
            if emit_lse:
                lse_all = jnp.where(lane == hd, m + jnp.log2(den), lse_all)
    if emit_lse:
        lse_ref[...] = lse_all


def band_attn(qkv, *, q_col, k_col, v_col, n_kv, group, half, tl, sink=None, emit_lse=False):
    nb, seq_len, _ = qkv.shape
    wq = n_kv * group * HEAD_DIM
    wk = n_kv * HEAD_DIM
    nq = seq_len // tl
    hb = tl // half
    nh = seq_len // half
    kern = functools.partial(_band_kernel, half=half, tl=tl, seq_len=seq_len, n_kv=n_kv, group=group,
                             has_sink=sink is not None, emit_lse=emit_lse)

    def halo(col):
        prev = pl.BlockSpec((None, half, wk), lambda b, i: (b, jnp.maximum(i * hb - 1, 0), col))
        cur = pl.BlockSpec((None, tl, wk), lambda b, i: (b, i, col))
        nxt = pl.BlockSpec((None, half, wk), lambda b, i: (b, jnp.minimum((i + 1) * hb, nh - 1), col))
        return [prev, cur, nxt]

    in_specs = [pl.BlockSpec((None, tl, wq), lambda b, i: (b, i, q_col))] + halo(k_col) + halo(v_col)
    args = [qkv] * 7
    if sink is not None:
        in_specs = [pl.BlockSpec(memory_space=pltpu.SMEM)] + in_specs
        args = [sink] + args
    out_spec = pl.BlockSpec((None, tl, wq), lambda b, i: (b, i, 0))
    o_shape = jax.ShapeDtypeStruct((nb, seq_len, wq), BF16)
    if emit_lse:
        out_shape = (o_shape, jax.ShapeDtypeStruct((nb, seq_len, LANES), F32))
        out_specs = (out_spec, pl.BlockSpec((None, tl, LANES), lambda b, i: (b, i, 0)))
    else:
        out_shape, out_specs = o_shape, out_spec
    return pl.pallas_call(
        kern,
        out_shape=out_shape,
        grid=(nb, nq),
        in_specs=in_specs,
        out_specs=out_specs,
        compiler_params=_params(("parallel", "arbitrary")),
        name="band_attn",
    )(*args)


def _outproj_kernel(a_ref, w_ref, gate_ref, *refs, n_a):
    x_refs, o_ref = refs[:-1], refs[-1]
    o_ref[...] = _read_rows(x_refs, n_a) + gate_ref[...] * _dot(a_ref[...], w_ref[...])


def outproj(a, w, x, mod3, gate_chunk, seq, *, tm, tn, weight_major=False):
    t, k = a.shape
    d = w.shape[1]
    tps = seq // tm
    nj = d // tn
    xs = _row_sources(x)
    if weight_major:
        assert len(xs) == 1
        grid = (nj, t // tm)
        order = lambda f: (lambda j, i: f(i, j))
    else:
        grid = (t // tm, nj)
        order = lambda f: f
    x_specs = _row_specs(xs, tm, tn, lambda j: j) if not weight_major else [
        pl.BlockSpec((tm, tn), lambda j, i: (i, j))]
    return pl.pallas_call(
        functools.partial(_outproj_kernel, n_a=xs[0].shape[0] // tm),
        out_shape=jax.ShapeDtypeStruct((t, d), F32),
        grid=grid,
        in_specs=[
            pl.BlockSpec((tm, k), order(lambda i, j: (i, 0))),
            pl.BlockSpec((k, tn), order(lambda i, j: (0, j))),
            pl.BlockSpec((None, 1, tn), order(lambda i, j: (i // tps, 0, gate_chunk * nj + j))),
        ] + x_specs,
        out_specs=pl.BlockSpec((tm, tn), order(lambda i, j: (i, j))),
        compiler_params=_params(("parallel", "arbitrary")),
        name="outproj",
    )(a, w, mod3, *xs)


def _merge_outproj_kernel(*refs, dils):
    n_g = len(dils)
    o_refs, l_refs = refs[:n_g], refs[n_g:2 * n_g]
    w_ref, x_ref, gate_ref, o_ref, a_scr = refs[2 * n_g:2 * n_g + 5]
    un_scr = refs[2 * n_g + 5:]
    j = pl.program_id(1)

    @pl.when(j == 0)
    def _():
        n_heads = a_scr.shape[1] // LANES
        slabs, ls = [], []
        si = 0
        for og_ref, lg_ref, dil in zip(o_refs, l_refs, dils):
            if dil == 1:
                slabs.append([og_ref[0, :, c * LANES:(c + 1) * LANES].astype(F32) for c in range(n_heads)])
                ls.append(lg_ref[0])
            else:
                so, sl = un_scr[si], un_scr[si + 1]
                si += 2
                n = og_ref.shape[1]
                for r in range(dil):
                    sl[pl.ds(r, n, stride=dil), :] = lg_ref[r]
                    for c in range(n_heads):
                        so[c, pl.ds(r, n, stride=dil), :] = og_ref[r, :, c * LANES:(c + 1) * LANES].astype(F32)
                slabs.append([so[c] for c in range(n_heads)])
                ls.append(sl[...])
        m = functools.reduce(jnp.maximum, ls)
        es = [jnp.exp2(l - m) for l in ls]
        den = functools.reduce(lambda a, b: a + b, es)
        alphas = [e / den for e in es]
        for c in range(n_heads):
            merged = functools.reduce(lambda a, b: a + b,
                                      [al[:, c:c + 1] * sg[c] for al, sg in zip(alphas, slabs)])
            a_scr[:, c * LANES:(c + 1) * LANES] = merged.astype(BF16)

    o_ref[...] = x_ref[...] + gate_ref[...] * _dot(a_scr[...], w_ref[...])


def merge_outproj(os_, lses, dils, w, x, mod3, gate_chunk, seq, *, tm, tn):
    k = os_[0].shape[-1]
    t, d = x.shape
    tps = seq // tm
    nj = d // tn
    grp = lambda dil, width: pl.BlockSpec((None, dil, tm // dil, width),
                                          lambda i, j: (i // tps, 0, i % tps, 0))
    un_scratch = []
    for dil in dils:
        if dil != 1:
            un_scratch += [pltpu.VMEM((k // LANES, tm, LANES), F32), pltpu.VMEM((tm, LANES), F32)]
    kern = functools.partial(_merge_outproj_kernel, dils=tuple(dils))
    return pl.pallas_call(
        kern,
        out_shape=jax.ShapeDtypeStruct((t, d), F32),
        grid=(t // tm, nj),
        in_specs=[grp(dil, k) for dil in dils] + [grp(dil, LANES) for dil in dils] + [
            pl.BlockSpec((k, tn), lambda i, j: (0, j)),
            pl.BlockSpec((tm, tn), lambda i, j: (i, j)),
            pl.BlockSpec((None, 1, tn), lambda i, j: (i // tps, 0, gate_chunk * nj + j)),
        ],
        out_specs=pl.BlockSpec((tm, tn), lambda i, j: (i, j)),
        scratch_shapes=[pltpu.VMEM((tm, k), BF16)] + un_scratch,
        compiler_params=_params(("parallel", "arbitrary")),
        name="merge_outproj",
    )(*os_, *lses, w, x, mod3)


def _silu_mul(g, u):
    return (g * jax.nn.sigmoid(g)) * u


def _ffn_up_kernel(x_ref, g_ref, sc_ref, sh_ref, wg_ref, wu_ref, o_ref, h_scr):
    j = pl.program_id(1)

    def project(h_rows):
        return _silu_mul(_dot(h_rows, wg_ref[...]), _dot(h_rows, wu_ref[...])).astype(BF16)

    @pl.when(j == 0)
    def _():
        g, sc, sh = g_ref[...], sc_ref[...], sh_ref[...]
        rc = x_ref.shape[0] // NORM_CHUNKS
        for c in range(NORM_CHUNKS):
            rs = slice(c * rc, (c + 1) * rc)
            hb = _norm_mod(x_ref[rs, :], g, sc, sh).astype(BF16)
            h_scr[rs, :] = hb
            o_ref[rs, :] = project(hb)

    @pl.when(j > 0)
    def _():
        o_ref[...] = project(h_scr[...])


def ffn_up(x, mod3, g, w_gu, seq, *, tm, tn):
    t, d = x.shape
    f = w_gu.shape[1] // 2
    tps = seq // tm
    nj = f // tn
    mod_spec = lambda k: pl.BlockSpec((None, 1, d), lambda i, j: (i // tps, 0, k))
    return pl.pallas_call(
        _ffn_up_kernel,
        out_shape=jax.ShapeDtypeStruct((t, f), BF16),
        grid=(t // tm, nj),
        in_specs=[
            pl.BlockSpec((tm, d), lambda i, j: (i, 0)),
            pl.BlockSpec((1, d), lambda i, j: (0, 0)),
            mod_spec(4), mod_spec(3),
            pl.BlockSpec((d, tn), lambda i, j: (0, j)),
            pl.BlockSpec((d, tn), lambda i, j: (0, nj + j)),
        ],
        out_specs=pl.BlockSpec((tm, tn), lambda i, j: (i, j)),
        scratch_shapes=[pltpu.VMEM((tm, d), BF16)],
        compiler_params=_params(("parallel", "arbitrary")),
        name="ffn_up",
    )(x, g.reshape(1, d), mod3, mod3, w_gu, w_gu)


MOE_TM = 512
R_E1, R_E2, R_W1, R_W2, R_RANK1, R_RANK2 = range(6)


def _moe_route_kernel(x_ref, g_ref, sc_ref, sh_ref, r_ref, tri_ref, route_ref, cnt_ref, cnt_scr):
    i = pl.program_id(0)

    @pl.when(i == 0)
    def _():
        cnt_scr[...] = jnp.zeros_like(cnt_scr)

    h = _norm_mod(x_ref[...], g_ref[...], sc_ref[...], sh_ref[...])
    logits = _dot3(h, r_ref[...])
    lane = lax.broadcasted_iota(jnp.int32, logits.shape, 1).astype(F32)
    logits = jnp.where(lane < N_EXPERTS, logits, NEG)
    v1 = jnp.max(logits, axis=1, keepdims=True)
    i1 = jnp.min(jnp.where(logits == v1, lane, float(LANES)), axis=1, keepdims=True)
    rest = jnp.where(lane == i1, NEG, logits)
    v2 = jnp.max(rest, axis=1, keepdims=True)
    i2 = jnp.min(jnp.where(rest == v2, lane, float(LANES)), axis=1, keepdims=True)
    e2 = jnp.exp(v2 - v1)
    w1 = 1.0 / (1.0 + e2)
    w2 = e2 / (1.0 + e2)
    sel = jnp.where(lane == i1, 1.0, 0.0) + jnp.where(lane == i2, 1.0, 0.0)
    base = cnt_scr[0:1, :]
    rank = _dot(tri_ref[...], sel.astype(BF16)) + base
    r1 = jnp.sum(jnp.where(lane == i1, rank, 0.0), axis=1, keepdims=True)
    r2 = jnp.sum(jnp.where(lane == i2, rank, 0.0), axis=1, keepdims=True)
    slab = jnp.zeros_like(logits)
    for col, val in ((R_E1, i1), (R_E2, i2), (R_W1, w1), (R_W2, w2), (R_RANK1, r1), (R_RANK2, r2)):
        slab = jnp.where(lane == col, val, slab)
    route_ref[...] = slab
    total = base + jnp.sum(sel, axis=0, keepdims=True)
    cnt_scr[0:1, :] = total
    cnt_ref[...] = jnp.broadcast_to(total, cnt_ref.shape)


def moe_route(x, mod3, g, router, seq, *, tm):
    t, d = x.shape
    tps = seq // tm
    mod_spec = lambda k: pl.BlockSpec((None, 1, d), lambda i: (i // tps, 0, k))
    router_p = jnp.pad(router, ((0, 0), (0, LANES - router.shape[1])))
    tri = (lax.broadcasted_iota(jnp.int32, (tm, tm), 1)
           < lax.broadcasted_iota(jnp.int32, (tm, tm), 0)).astype(BF16)
    return pl.pallas_call(
        _moe_route_kernel,
        out_shape=(jax.ShapeDtypeStruct((t, LANES), F32), jax.ShapeDtypeStruct((8, LANES), F32)),
        grid=(t // tm,),
        in_specs=[
            pl.BlockSpec((tm, d), lambda i: (i, 0)),
            pl.BlockSpec((1, d), lambda i: (0, 0)),
            mod_spec(4), mod_spec(3),
            pl.BlockSpec((d, LANES), lambda i: (0, 0)),
            pl.BlockSpec((tm, tm), lambda i: (0, 0)),
        ],
        out_specs=(pl.BlockSpec((tm, LANES), lambda i: (i, 0)), pl.BlockSpec((8, LANES), lambda i: (0, 0))),
        scratch_shapes=[pltpu.VMEM((8, LANES), F32)],
        compiler_params=_params(("arbitrary",)),
        name="moe_route",
    )(x, g.reshape(1, d), mod3, mod3, router_p, tri)


def _moe_plan(route, counts, n_tok, tm):
    n_tiles_max = 2 * n_tok // MOE_TM + N_EXPERTS
    cnt = counts[0, :N_EXPERTS].astype(jnp.int32)
    tiles = (cnt + MOE_TM - 1) // MOE_TM
    tile_end = jnp.cumsum(tiles)
    tile_start = tile_end - tiles
    row_start = tile_start * MOE_TM
    n_tiles = tile_end[-1]
    e1 = route[:, R_E1].astype(jnp.int32)
    e2 = route[:, R_E2].astype(jnp.int32)
    pos1 = row_start[e1] + route[:, R_RANK1].astype(jnp.int32)
    pos2 = row_start[e2] + route[:, R_RANK2].astype(jnp.int32)
    shape = (n_tok // tm, 1, tm)
    tile_id = jnp.arange(n_tiles_max, dtype=jnp.int32)
    tile_expert = jnp.minimum(jnp.sum((tile_id[:, None] >= tile_end[None, :]).astype(jnp.int32), axis=1),
                              N_EXPERTS - 1)
    last_tile = jnp.where(tiles > 0, tile_end - 1, -1)
    tail_tile = n_tiles + jnp.arange(N_EXPERTS, dtype=jnp.int32)
    tail_tile = jnp.where(tail_tile < n_tiles_max, tail_tile, -1)
    zero_tiles = jnp.concatenate([last_tile, tail_tile])
    zero_rows = jnp.where(zero_tiles >= 0, zero_tiles * MOE_TM, -1).astype(jnp.int32)
    return pos1.reshape(shape), pos2.reshape(shape), tile_expert.astype(jnp.int32), zero_rows, n_tiles_max


def _moe_dispatch_kernel(zero_ref, pos1_ref, pos2_ref, x_ref, g_ref, sc_ref, sh_ref, xs_ref,
                         h_scr, zero_scr, sem, zsem):
    i = pl.program_id(0)
    tm = x_ref.shape[0]

    @pl.when(i == 0)
    def _():
        zero_scr[...] = jnp.zeros_like(zero_scr)
        for z in range(2 * N_EXPERTS):
            @pl.when(zero_ref[z] >= 0)
            def _():
                row = pl.multiple_of(zero_ref[z], MOE_TM)
                pltpu.make_async_copy(zero_scr, xs_ref.at[pl.ds(row, MOE_TM), :], zsem).start()
        for z in range(2 * N_EXPERTS):
            @pl.when(zero_ref[z] >= 0)
            def _():
                pltpu.make_async_copy(zero_scr, xs_ref.at[pl.ds(0, MOE_TM), :], zsem).wait()

    h_scr[...] = _norm_mod(x_ref[...], g_ref[...], sc_ref[...], sh_ref[...])

    def issue(r, carry):
        src = h_scr.at[pl.ds(r, 1), :]
        pltpu.make_async_copy(src, xs_ref.at[pl.ds(pos1_ref[0, r], 1), :], sem).start()
        pltpu.make_async_copy(src, xs_ref.at[pl.ds(pos2_ref[0, r], 1), :], sem).start()
        return carry

    lax.fori_loop(0, tm, issue, 0, unroll=8)
    for _ in range(2):
        pltpu.make_async_copy(h_scr, xs_ref.at[pl.ds(0, tm), :], sem).wait()


def moe_dispatch(x, mod3, g, pos1, pos2, zero_rows, n_rows, seq, *, tm):
    t, d = x.shape
    tps = seq // tm
    mod_spec = lambda k: pl.BlockSpec((None, 1, d), lambda i, *_: (i // tps, 0, k))
    pos_spec = pl.BlockSpec((None, 1, tm), lambda i, *_: (i, 0, 0), memory_space=pltpu.SMEM)
    return pl.pallas_call(
        _moe_dispatch_kernel,
        out_shape=jax.ShapeDtypeStruct((n_rows, d), F32),
        grid_spec=pltpu.PrefetchScalarGridSpec(
            num_scalar_prefetch=1,
            grid=(t // tm,),
            in_specs=[
                pos_spec, pos_spec,
                pl.BlockSpec((tm, d), lambda i, *_: (i, 0)),
                pl.BlockSpec((1, d), lambda i, *_: (0, 0)),
                mod_spec(4), mod_spec(3),
            ],
            out_specs=pl.BlockSpec(memory_space=pl.ANY),
            scratch_shapes=[pltpu.VMEM((tm, d), F32), pltpu.VMEM((MOE_TM, d), F32),
                            pltpu.SemaphoreType.DMA, pltpu.SemaphoreType.DMA],
        ),
        compiler_params=_params(("arbitrary",)),
        name="moe_dispatch",
    )(zero_rows, pos1, pos2, x, g.reshape(1, d), mod3, mod3)


def _moe_expert_kernel(expert_ref, xs_ref, wgu_ref, wd_ref, ys_ref):
    del expert_ref
    f = wd_ref.shape[0]
    gu = _dot(xs_ref[...].astype(BF16), wgu_ref[...])
    act = _silu_mul(gu[:, :f], gu[:, f:]).astype(BF16)
    ys_ref[...] = _dot(act, wd_ref[...])


def moe_expert(xs, w_gu, w_down, tile_expert):
    n_rows, d = xs.shape
    _, _, two_f = w_gu.shape
    return pl.pallas_call(
        _moe_expert_kernel,
        out_shape=jax.ShapeDtypeStruct((n_rows, d), F32),
        grid_spec=pltpu.PrefetchScalarGridSpec(
            num_scalar_prefetch=1,
            grid=(n_rows // MOE_TM,),
            in_specs=[
                pl.BlockSpec((MOE_TM, d), lambda s, ex: (s, 0)),
                pl.BlockSpec((None, d, two_f), lambda s, ex: (ex[s], 0, 0)),
                pl.BlockSpec((None, two_f // 2, d), lambda s, ex: (ex[s], 0, 0)),
            ],
            out_specs=pl.BlockSpec((MOE_TM, d), lambda s, ex: (s, 0)),
        ),
        compiler_params=_params(("arbitrary",)),
        name="moe_expert",
    )(tile_expert, xs, w_gu, w_down)


def _moe_combine_kernel(pos1_ref, pos2_ref, npos1_ref, npos2_ref, x_ref, route_ref, gate_ref, ys_ref,
                        *refs, n_a):
    if n_a is None:
        o_ref, y1_scr, y2_scr, sem = refs
    else:
        fg_ref, oa_ref, ob_ref, y1_scr, y2_scr, sem = refs
    tm = x_ref.shape[0]
    i = pl.program_id(0)
    slot = i % 2

    def gather(p1_ref, p2_ref, s):
        def issue(r, carry):
            pltpu.make_async_copy(ys_ref.at[pl.ds(p1_ref[0, r], 1), :], y1_scr.at[s, pl.ds(r, 1), :],
                                  sem.at[s]).start()
            pltpu.make_async_copy(ys_ref.at[pl.ds(p2_ref[0, r], 1), :], y2_scr.at[s, pl.ds(r, 1), :],
                                  sem.at[s]).start()
            return carry

        lax.fori_loop(0, tm, issue, 0, unroll=8)

    @pl.when(i == 0)
    def _():
        gather(pos1_ref, pos2_ref, 0)

    @pl.when(i + 1 < pl.num_programs(0))
    def _():
        gather(npos1_ref, npos2_ref, 1 - slot)

    for buf in (y1_scr, y2_scr):
        pltpu.make_async_copy(ys_ref.at[pl.ds(0, tm), :], buf.at[slot], sem.at[slot]).wait()
    route = route_ref[...]
    w1 = route[:, R_W1:R_W1 + 1]
    w2 = route[:, R_W2:R_W2 + 1]
    x = x_ref[...] + gate_ref[...] * (w1 * y1_scr[slot] + w2 * y2_scr[slot])
    if n_a is None:
        o_ref[...] = x
        return
    ms = jnp.mean(x * x, axis=-1, keepdims=True)
    y = (x * lax.rsqrt(ms + NORM_EPS)) * fg_ref[...]
    i = pl.program_id(0)

    @pl.when(i < n_a)
    def _():
        oa_ref[...] = y

    @pl.when(i >= n_a)
    def _():
        ob_ref[...] = y


def moe_combine(x, mod3, route, pos1, pos2, ys, seq, *, tm, final=None):
    t, d = x.shape
    tps = seq // tm
    last = t // tm - 1
    pos_spec = pl.BlockSpec((None, 1, tm), lambda i: (i, 0, 0), memory_space=pltpu.SMEM)
    next_spec = pl.BlockSpec((None, 1, tm), lambda i: (jnp.minimum(i + 1, last), 0, 0),
                             memory_space=pltpu.SMEM)
    in_specs = [
        pos_spec, pos_spec, next_spec, next_spec,
        pl.BlockSpec((tm, d), lambda i: (i, 0)),
        pl.BlockSpec((tm, LANES), lambda i: (i, 0)),
        pl.BlockSpec((None, 1, d), lambda i: (i // tps, 0, 5)),
        pl.BlockSpec(memory_space=pl.ANY),
    ]
    args = [pos1, pos2, pos1, pos2, x, route, mod3, ys]
    if final is None:
        n_a = None
        out_shape = jax.ShapeDtypeStruct((t, d), F32)
        out_specs = pl.BlockSpec((tm, d), lambda i: (i, 0))
    else:
        final_g, rows_a = final
        n_a = rows_a // tm
        in_specs.append(pl.BlockSpec((1, d), lambda i: (0, 0)))
        args.append(final_g.reshape(1, d))
        out_shape = (jax.ShapeDtypeStruct((rows_a, d), F32), jax.ShapeDtypeStruct((t - rows_a, d), F32))
        out_specs = (pl.BlockSpec((tm, d), lambda i: (jnp.minimum(i, n_a - 1), 0)),
                     pl.BlockSpec((tm, d), lambda i: (jnp.maximum(i - n_a, 0), 0)))
    return pl.pallas_call(
        functools.partial(_moe_combine_kernel, n_a=n_a),
        out_shape=out_shape,
        grid=(t // tm,),
        in_specs=in_specs,
        out_specs=out_specs,
        scratch_shapes=[pltpu.VMEM((2, tm, d), F32), pltpu.VMEM((2, tm, d), F32),
                        pltpu.SemaphoreType.DMA((2,))],
        compiler_params=_params(("arbitrary",)),
        name="moe_combine",
    )(*args)


def moe_ffn(x, mod3, g, router, w_gu, w_down, seq, *, tm, final=None):
    t = x.shape[0]
    route, counts = moe_route(x, mod3, g, router, seq, tm=tm)
    pos1, pos2, tile_expert, zero_rows, n_tiles_max = _moe_plan(route, counts, t, tm)
    xs = moe_dispatch(x, mod3, g, pos1, pos2, zero_rows, n_tiles_max * MOE_TM, seq, tm=tm)
    ys = moe_expert(xs, w_gu, w_down, tile_expert)
    return moe_combine(x, mod3, route, pos1, pos2, ys, seq, tm=tm, final=final)


def _lambda_init(layer):
    return 0.8 - 0.6 * math.exp(-0.3 * layer)


def _mixer_a(x, mod3, g, w_in, w_out, lam_p, subln, tables, nb, seq, layer, cfg):
    tm_in = cfg["tm"] if len(_row_sources(x)) > 1 else cfg["tm_in"]
    qkv = inproj(x, mod3, g, w_in, tables, seq, q_cols=D_MODEL, rope_cols=2 * D_MODEL,
                 tm=tm_in, tn=cfg["tn_in"]).reshape(nb * seq, 3 * D_MODEL)
    o = attn_a(qkv, lam_p, subln, nb, seq, _lambda_init(layer), tq=cfg["tq_a"], tk=cfg["tk_a"],
               tk_pv=cfg["tk_pv_a"], unroll=cfg["unroll_a"])
    return outproj(o, w_out, x, mod3, 2, seq, tm=cfg["tm"], tn=cfg["tn_out_mix"])


def _mixer_b(x, mod3, g, w_in, w_out, sink, tables, nb, seq, cfg):
    nq = B_HEADS * HEAD_DIM
    qkv = inproj(x, mod3, g, w_in, tables, seq, q_cols=nq, rope_cols=nq + B_KV_HEADS * HEAD_DIM,
                 tm=cfg["tm_in"], tn=cfg["tn_in_b"]).reshape(nb, seq, -1)
    o = band_attn(qkv, q_col=0, k_col=4, v_col=5, n_kv=B_KV_HEADS, group=B_HEADS // B_KV_HEADS,
                  half=B_WINDOW, tl=cfg["tl_b"], sink=sink)
    return outproj(o.reshape(nb * seq, nq), w_out, x, mod3, 2, seq, tm=cfg["tm"], tn=cfg["tn_out_mix"])


def _mixer_c(x, mod3, g, w_in, w_out, tables, nb, seq, cfg):
    n_groups = len(C_CONFIGS)
    os_, lses, dils = [], [], []
    for gi, (window, dil) in enumerate(C_CONFIGS):
        qkv = inproj(x, mod3, g, w_in, tables, seq, q_cols=C_WIDTH, rope_cols=2 * C_WIDTH,
                     tm=cfg["tm_in"], tn=C_WIDTH, dil=dil, col_stride=n_groups, col_off=gi,
                     n_cols=3 * C_WIDTH)
        sub = seq // dil
        o, lse = band_attn(qkv.reshape(nb * dil, sub, 3 * C_WIDTH), q_col=0, k_col=1, v_col=2,
                           n_kv=C_HEADS, group=1, half=window // (2 * dil), tl=cfg["tl_c"],
                           emit_lse=True)
        os_.append(o.reshape(nb, dil, sub, C_WIDTH))
        lses.append(lse.reshape(nb, dil, sub, LANES))
        dils.append(dil)
    return merge_outproj(os_, lses, dils, w_out, x, mod3, 2, seq, tm=cfg["tm_merge"],
                         tn=cfg["tn_out_ffn"])


def _default_cfg(seq):
    return dict(tm=min(512, seq), tm_in=min(1024, seq), tm_up=min(1024, seq), tn_in=1024, tn_in_b=512,
                tn_out_mix=D_MODEL, tn_out_ffn=1024,
                tq_a=min(512, seq), tk_a=min(512, seq), tl_b=min(256, seq),
                tl_c=min(128, seq // C_CONFIGS[-1][1]), tm_merge=min(512, seq), tn_up=512,
                unroll_a=True, tk_pv_a=min(2048, seq))


def trunk_all(x, c8, nb, seq, ada_w, ada_b, norm_mix, norm_ffn, a_w_in, a_w_out, a_lambda, a_subln,
              b_w_in, b_w_out, b_sink, c_w_in, c_w_out, f_w_gu, f_w_down, moe_router, moe_w_gu,
              moe_w_down, final_norm_g, cfg=None):
    assert (DEPTH - 1) % 2 == 1, "the final norm is fused into the last layer's routed-FFN combine"
    rows_a = x[0].shape[0]
    cfg = cfg or _default_cfg(seq)
    bf = lambda w: w.astype(BF16)
    mod_all = adaln(c8, ada_w, ada_b)
    tables = _rope_tables(seq)
    for i in range(DEPTH):
        mod3 = mod_all[i].reshape(8, 1, 6 * D_MODEL)
        kind, jm = i % 3, i // 3
        if kind == 0:
            x = _mixer_a(x, mod3, norm_mix[i], bf(a_w_in[jm]), bf(a_w_out[jm]), a_lambda[jm],
                         a_subln[jm], tables, nb, seq, i, cfg)
        elif kind == 1:
            x = _mixer_b(x, mod3, norm_mix[i], bf(b_w_in[jm]), bf(b_w_out[jm]), b_sink[jm], tables,
                         nb, seq, cfg)
        else:
            x = _mixer_c(x, mod3, norm_mix[i], bf(c_w_in[jm]), bf(c_w_out[jm]), tables, nb, seq, cfg)
        jf = i // 2
        if i % 2 == 0:
            act = ffn_up(x, mod3, norm_ffn[i], bf(f_w_gu[jf]), seq, tm=cfg["tm_up"], tn=cfg["tn_up"])
            x = outproj(act, bf(f_w_down[jf]), x, mod3, 5, seq, tm=cfg["tm"], tn=cfg["tn_out_ffn"],
                        weight_major=True)
        else:
            x = moe_ffn(x, mod3, norm_ffn[i], moe_router[jf], bf(moe_w_gu[jf]), bf(moe_w_down[jf]), seq,
                        tm=cfg["tm"], final=(final_norm_g, rows_a) if i == DEPTH - 1 else None)
    return x


def kernel(x_prompt, x_sample, c_prompt, c_sample, ada_w, ada_b, norm_mix, norm_ffn, a_w_in, a_w_out,
           a_lambda, a_subln, b_w_in, b_w_out, b_sink, c_w_in, c_w_out, f_w_gu, f_w_down, moe_router,
           moe_w_gu, moe_w_down, final_norm):
    bp, seq, d = x_prompt.shape
    bs = x_sample.shape[0]
    assert x_sample.shape[1] == seq
    nb = bp + bs
    x = (x_prompt.reshape(bp * seq, d), x_sample.reshape(bs * seq, d))
    c8 = jnp.concatenate([c_prompt, c_sample, jnp.zeros((8 - nb, d), F32)], axis=0)
    ya, yb = trunk_all(x, c8, nb, seq, ada_w, ada_b, norm_mix, norm_ffn, a_w_in, a_w_out, a_lambda,
                       a_subln, b_w_in, b_w_out, b_sink, c_w_in, c_w_out, f_w_gu, f_w_down, moe_router,
                       moe_w_gu, moe_w_down, final_norm)
    return (ya.reshape(bp, seq, d), yb.reshape(bs, seq, d))
```

```python
import functools
import math

import jax
import jax.numpy as jnp
from jax import lax
from jax.experimental import pallas as pl
from jax.experimental.pallas import tpu as pltpu

F32 = jnp.float32
BF16 = jnp.bfloat16

D_MODEL = 2048
DEPTH = 4
HEAD_DIM = 128
ROPE_THETA = 500000.0
ROT = HEAD_DIM // 4
NORM_EPS = 1e-6
NEG = -1e30
A_HEADS = 8
B_HEADS = 16
B_KV_HEADS = 4
B_WINDOW = 128
C_CONFIGS = ((128, 1), (512, 4), (2048, 16))
C_HEADS = 8
C_WIDTH = C_HEADS * HEAD_DIM
N_EXPERTS = 8

LOG2E = math.log2(math.e)
Q_SCALE = HEAD_DIM ** -0.5 * LOG2E

LANES = 128
VMEM_LIMIT = 56 * 1024 * 1024
NORM_CHUNKS = 4

NT_DIMS = (((1,), (1,)), ((), ()))


def _params(sem, vmem=VMEM_LIMIT):
    return pltpu.CompilerParams(dimension_semantics=sem, vmem_limit_bytes=vmem)


def _dot(a, b):
    return jnp.dot(a, b, preferred_element_type=F32)


def _split_bf16(x):
    hi = x.astype(BF16)
    lo = (x - hi.astype(F32)).astype(BF16)
    return hi, lo


def _dot3(a, b):
    a_hi, a_lo = _split_bf16(a)
    b_hi, b_lo = _split_bf16(b)
    return _dot(a_hi, b_hi) + (_dot(a_hi, b_lo) + _dot(a_lo, b_hi))


def _row_sources(x):
    return tuple(x) if isinstance(x, (tuple, list)) else (x,)


def _row_specs(xs, tm, width, col_of):
    if len(xs) == 1:
        return [pl.BlockSpec((tm, width), lambda i, *a: (i, col_of(*a)))]
    n_a = xs[0].shape[0] // tm
    return [pl.BlockSpec((tm, width), lambda i, *a: (jnp.minimum(i, n_a - 1), col_of(*a))),
            pl.BlockSpec((tm, width), lambda i, *a: (jnp.maximum(i - n_a, 0), col_of(*a)))]


def _read_rows(x_refs, n_a, rs=slice(None)):
    if len(x_refs) == 1:
        return x_refs[0][rs, :]
    return jnp.where(pl.program_id(0) < n_a, x_refs[0][rs, :], x_refs[1][rs, :])


def _norm_mod(x, g, sc, sh):
    ms = jnp.mean(x * x, axis=-1, keepdims=True)
    y = x * lax.rsqrt(ms + NORM_EPS)
    return (y * g) * (1.0 + sc) + sh


def _adaln_kernel(c_ref, w_ref, b_ref, o_ref):
    c = c_ref[...]
    cs = c * jax.nn.sigmoid(c)
    o_ref[...] = _dot3(cs, w_ref[...]) + b_ref[...]


def adaln(c8, ada_w, ada_b, tn=512):
    depth, d, n = ada_w.shape
    return pl.pallas_call(
        _adaln_kernel,
        out_shape=jax.ShapeDtypeStruct((depth, 8, n), F32),
        grid=(depth, n // tn),
        in_specs=[
            pl.BlockSpec((8, d), lambda l, j: (0, 0)),
            pl.BlockSpec((None, d, tn), lambda l, j: (l, 0, j)),
            pl.BlockSpec((None, 1, tn), lambda l, j: (l, 0, j)),
        ],
        out_specs=pl.BlockSpec((None, 8, tn), lambda l, j: (l, 0, j)),
        compiler_params=_params(("arbitrary", "arbitrary")),
        name="adaln",
    )(c8, ada_w, ada_b.reshape(depth, 1, n))


def _rope_tables(seq):
    pos = jnp.arange(seq, dtype=F32)
    half = ROT // 2
    inv = ROPE_THETA ** (-jnp.arange(half, dtype=F32) * (2.0 / ROT))
    ang = pos[:, None] * inv[None, :]
    cos, sin = jnp.cos(ang), jnp.sin(ang)
    zeros = lambda n: jnp.zeros((seq, n), F32)
    cos_t = jnp.concatenate([cos, cos, jnp.ones((seq, HEAD_DIM - ROT), F32)], axis=1)
    sin_a = jnp.concatenate([zeros(half), sin, zeros(HEAD_DIM - ROT)], axis=1)
    sin_b = jnp.concatenate([-sin, zeros(HEAD_DIM - half)], axis=1)
    return cos_t, sin_a, sin_b


def _perm_tables(tables, dil):
    return tuple(t.reshape(-1, dil, HEAD_DIM).transpose(1, 0, 2) for t in tables)


def _inproj_kernel(*refs, n_x, n_a, n_q, n_rope, tn, dil):
    x_refs = refs[:n_x]
    g_ref, sc_ref, sh_ref, w_ref, cos_ref, sa_ref, sb_ref, o_ref, h_scr = refs[n_x:n_x + 9]
    maybe_slab_scr = refs[n_x + 9:]
    j = pl.program_id(1)
    tm, d = x_refs[0].shape
    n = tm // dil
    assert n_q >= 1

    def project(h_rows, rr, ll, scale):
        shape = (rr.stop - rr.start, ll.stop - ll.start, HEAD_DIM)
        cos, sa, sb = (ref[rr, ll, :].reshape(-1, HEAD_DIM) * scale for ref in (cos_ref, sa_ref, sb_ref))
        acc = _dot(h_rows, w_ref[...])
        for hh in range(tn // HEAD_DIM):
            cols = slice(hh * HEAD_DIM, (hh + 1) * HEAD_DIM)
            a = acc[:, cols]
            r = a * cos + pltpu.roll(a, ROT // 2, 1) * sa + pltpu.roll(a, HEAD_DIM - ROT // 2, 1) * sb
            o_ref[rr, ll, cols] = r.reshape(shape).astype(BF16)

    whole = (slice(0, dil), slice(0, n))

    @pl.when(j == 0)
    def _():
        g, sc, sh = g_ref[...], sc_ref[...], sh_ref[...]
        if dil == 1:
            rc = tm // NORM_CHUNKS
            for c in range(NORM_CHUNKS):
                rs = slice(c * rc, (c + 1) * rc)
                hb = _norm_mod(_read_rows(x_refs, n_a, rs), g, sc, sh).astype(BF16)
                h_scr[rs, :] = hb
                project(hb, slice(0, 1), rs, Q_SCALE)
        else:
            h = _norm_mod(_read_rows(x_refs, n_a), g, sc, sh)
            (slab_scr,) = maybe_slab_scr
            for c in range(d // LANES):
                slab_scr[c] = h[:, c * LANES:(c + 1) * LANES]
            per = dil // NORM_CHUNKS
            for k in range(NORM_CHUNKS):
                for r in range(k * per, (k + 1) * per):
                    for c in range(d // LANES):
                        rows = slab_scr[c, pl.ds(r, n, stride=dil), :]
                        h_scr[r * n:(r + 1) * n, c * LANES:(c + 1) * LANES] = rows.astype(BF16)
                project(h_scr[k * per * n:(k + 1) * per * n, :], slice(k * per, (k + 1) * per),
                        slice(0, n), Q_SCALE)

    @pl.when(jnp.logical_and(j > 0, j < n_rope))
    def _():
        project(h_scr[...], *whole, jnp.where(j < n_q, Q_SCALE, 1.0).astype(F32))

    @pl.when(j >= n_rope)
    def _():
        o_ref[...] = _dot(h_scr[...], w_ref[...]).reshape(dil, n, tn).astype(BF16)


def inproj(x, mod3, g, w, tables, seq, *, q_cols, rope_cols, tm, tn, dil=1, col_stride=1, col_off=0,
           n_cols=None):
    xs = _row_sources(x)
    t = sum(a.shape[0] for a in xs)
    d = xs[0].shape[1]
    n_cols = n_cols or w.shape[1]
    tps = seq // tm
    n = tm // dil
    kern = functools.partial(_inproj_kernel, n_x=len(xs), n_a=xs[0].shape[0] // tm, n_q=q_cols // tn,
                             n_rope=rope_cols // tn, tn=tn, dil=dil)
    mod_spec = lambda k: pl.BlockSpec((None, 1, d), lambda i, j: (i // tps, 0, k))
    tab_spec = pl.BlockSpec((dil, n, HEAD_DIM), lambda i, j: (0, i % tps, 0))
    return pl.pallas_call(
        kern,
        out_shape=jax.ShapeDtypeStruct((t // seq, dil, seq // dil, n_cols), BF16),
        grid=(t // tm, n_cols // tn),
        in_specs=_row_specs(xs, tm, d, lambda j: 0) + [
            pl.BlockSpec((1, d), lambda i, j: (0, 0)),
            mod_spec(1), mod_spec(0),
            pl.BlockSpec((d, tn), lambda i, j: (0, j * col_stride + col_off)),
            tab_spec, tab_spec, tab_spec,
        ],
        out_specs=pl.BlockSpec((None, dil, n, tn), lambda i, j: (i // tps, 0, i % tps, j)),
        scratch_shapes=[pltpu.VMEM((tm, d), BF16)]
        + ([pltpu.VMEM((d // LANES, tm, LANES), F32)] if dil > 1 else []),
        compiler_params=_params(("parallel", "arbitrary")),
        name="inproj",
    )(*xs, g.reshape(1, d), mod3, mod3, w, *_perm_tables(tables, dil))


L_SAFE = 2.0 ** -64


def _attn_a_kernel(q_ref, k_ref, v_ref, lam_ref, subln_ref, o_ref, p_scr, knorm_scr,
                   *, tk, tk_pv, lam0, unroll):
    i = pl.program_id(2)
    seq = k_ref.shape[0]
    tq = q_ref.shape[0]
    dv = v_ref.shape[1]
    n_chunks = seq // tk

    @pl.when(i == 0)
    def _():
        def body(c, carry):
            off = pl.multiple_of(c * tk, tk)
            kf = k_ref[pl.ds(off, tk), :].astype(F32)
            sq = kf * kf
            n0 = jnp.max(jnp.sum(sq[:, :HEAD_DIM], axis=1, keepdims=True), axis=0, keepdims=True)
            n1 = jnp.max(jnp.sum(sq[:, HEAD_DIM:], axis=1, keepdims=True), axis=0, keepdims=True)
            return jnp.maximum(carry[0], n0), jnp.maximum(carry[1], n1)

        zero = jnp.zeros((1, 1), F32)
        n0, n1 = lax.fori_loop(0, n_chunks, body, (zero, zero))
        knorm_scr[0:1, :] = jnp.broadcast_to(jnp.sqrt(n0), (1, LANES))
        knorm_scr[1:2, :] = jnp.broadcast_to(jnp.sqrt(n1), (1, LANES))

    q = q_ref[...]
    qs = (q[:, :HEAD_DIM], q[:, HEAD_DIM:])
    lp = lam_ref[...].astype(F32)
    lam = (jnp.exp(jnp.sum(lp[0:1] * lp[1:2], axis=1, keepdims=True))
           - jnp.exp(jnp.sum(lp[2:3] * lp[3:4], axis=1, keepdims=True)) + lam0)

    def finish(o):
        ms = jnp.mean(o * o, axis=1, keepdims=True)
        y = (o * lax.rsqrt(ms + NORM_EPS)) * subln_ref[...]
        o_ref[...] = (y * (1.0 - lam0)).astype(BF16)

    ls = []
    for mp in range(2):
        qc = qs[mp]
        qf = qc.astype(F32)
        qn = jnp.sqrt(jnp.sum(qf * qf, axis=1, keepdims=True))
        shift = qn * (knorm_scr[mp:mp + 1, :] * 1.001) + 1.0
        cols = slice(mp * HEAD_DIM, (mp + 1) * HEAD_DIM)

        def chunk(c, lpart, mp=mp, qc=qc, shift=shift, cols=cols):
            off = pl.multiple_of(c * tk, tk)
            s = lax.dot_general(qc, k_ref[pl.ds(off, tk), cols], NT_DIMS, preferred_element_type=F32)
            for g in range(tk // LANES):
                p = jnp.exp2(s[:, g * LANES:(g + 1) * LANES] - shift)
                lpart = lpart + p
                p_scr[mp, :, pl.ds(pl.multiple_of(off + g * LANES, LANES), LANES)] = p
            return lpart

        lpart = lax.fori_loop(0, n_chunks, chunk, jnp.zeros((tq, LANES), F32), unroll=unroll)
        ls.append(jnp.sum(lpart, axis=1, keepdims=True))

    l_min = jnp.min(jnp.minimum(ls[0], ls[1]))

    @pl.when(l_min >= L_SAFE)
    def _():
        ratio = jnp.broadcast_to(lam * ls[0] / ls[1], (tq, LANES))
        acc = jnp.zeros((tq, dv), F32)
        for c in range(seq // tk_pv):
            groups = []
            for g in range(tk_pv // LANES):
                cs = slice(c * tk_pv + g * LANES, c * tk_pv + (g + 1) * LANES)
                groups.append((p_scr[0, :, cs] - ratio * p_scr[1, :, cs]).astype(BF16))
            a = jnp.concatenate(groups, axis=1)
            acc = acc + _dot(a, v_ref[c * tk_pv:(c + 1) * tk_pv, :])
        finish(acc / ls[0])

    @pl.when(l_min < L_SAFE)
    def _():
        def update(qc, kc, v, m, l, acc):
            s = lax.dot_general(qc, kc, NT_DIMS, preferred_element_type=F32)
            m_new = jnp.maximum(m, jnp.max(s, axis=1, keepdims=True))
            alpha = jnp.exp2(m - m_new)
            p = jnp.exp2(s - m_new)
            l = alpha * l + jnp.sum(p, axis=1, keepdims=True)
            acc = alpha * acc + _dot(p.astype(BF16), v)
            return m_new, l, acc

        def body(c, carry):
            m0, l0, a0, m1, l1, a1 = carry
            off = pl.multiple_of(c * tk, tk)
            k = k_ref[pl.ds(off, tk), :]
            v = v_ref[pl.ds(off, tk), :]
            m0, l0, a0 = update(qs[0], k[:, :HEAD_DIM], v, m0, l0, a0)
            m1, l1, a1 = update(qs[1], k[:, HEAD_DIM:], v, m1, l1, a1)
            return m0, l0, a0, m1, l1, a1

        m_init = jnp.full((tq, 1), NEG, F32)
        l_init = jnp.zeros((tq, 1), F32)
        a_init = jnp.zeros((tq, dv), F32)
        _, l0, a0, _, l1, a1 = lax.fori_loop(
            0, n_chunks, body, (m_init, l_init, a_init, m_init, l_init, a_init))
        finish(a0 / l0 - lam * (a1 / l1))


def attn_a(qkv, lam_p, subln, nb, seq, lam0, *, tq, tk, tk_pv, unroll):
    t = qkv.shape[0]
    dv = 2 * HEAD_DIM
    nq = seq // tq
    kern = functools.partial(_attn_a_kernel, tk=tk, tk_pv=tk_pv, lam0=lam0, unroll=unroll)
    return pl.pallas_call(
        kern,
        out_shape=jax.ShapeDtypeStruct((t, A_HEADS * dv), BF16),
        grid=(nb, A_HEADS, nq),
        in_specs=[
            pl.BlockSpec((tq, dv), lambda b, h, i: (b * nq + i, h)),
            pl.BlockSpec((seq, dv), lambda b, h, i: (b, A_HEADS + h), pipeline_mode=pl.Buffered(1)),
            pl.BlockSpec((seq, dv), lambda b, h, i: (b, 2 * A_HEADS + h), pipeline_mode=pl.Buffered(1)),
            pl.BlockSpec((4, HEAD_DIM), lambda b, h, i: (0, 0)),
            pl.BlockSpec((1, dv), lambda b, h, i: (0, 0)),
        ],
        out_specs=pl.BlockSpec((tq, dv), lambda b, h, i: (b * nq + i, h)),
        scratch_shapes=[pltpu.VMEM((2, tq, seq), F32), pltpu.VMEM((8, LANES), F32)],
        compiler_params=_params(("parallel", "parallel", "arbitrary")),
        name="attn_a",
    )(qkv, qkv, qkv, lam_p, subln.reshape(1, dv))


def _band_kernel(*refs, half, tl, seq_len, n_kv, group, has_sink, emit_lse):
    if has_sink:
        sink_ref, refs = refs[0], refs[1:]
    q_ref, kp_ref, kc_ref, kn_ref, vp_ref, vc_ref, vn_ref = refs[:7]
    o_ref = refs[7]
    lse_ref = refs[8] if emit_lse else None
    i = pl.program_id(1)
    win = tl + 2 * half
    qpos = i * tl + lax.broadcasted_iota(jnp.int32, (tl, win), 0)
    kpos = i * tl - half + lax.broadcasted_iota(jnp.int32, (tl, win), 1)
    dist = jnp.abs(kpos - qpos)
    dist = jnp.where(kpos < 0, win, dist)
    dist = jnp.where(kpos >= seq_len, win, dist)
    bias = jnp.where(dist <= half, 0.0, NEG)
    lane = lax.broadcasted_iota(jnp.int32, (tl, LANES), 1)
    lse_all = jnp.zeros((tl, LANES), F32)
    for kv in range(n_kv):
        cs = slice(kv * HEAD_DIM, (kv + 1) * HEAD_DIM)
        kw = jnp.concatenate([kp_ref[:, cs], kc_ref[:, cs], kn_ref[:, cs]], axis=0)
        vw = jnp.concatenate([vp_ref[:, cs], vc_ref[:, cs], vn_ref[:, cs]], axis=0)
        for g in range(group):
            hd = kv * group + g
            hs = slice(hd * HEAD_DIM, (hd + 1) * HEAD_DIM)
            s = lax.dot_general(q_ref[:, hs], kw, NT_DIMS, preferred_element_type=F32) + bias
            m = jnp.max(s, axis=1, keepdims=True)
            if has_sink:
                sk = sink_ref[hd] * LOG2E
                m = jnp.maximum(m, sk)
            e = jnp.exp2(s - m)
            den = jnp.sum(e, axis=1, keepdims=True)
            if has_sink:
                den = den + jnp.exp2(sk - m)
            o = _dot(e.astype(BF16), vw) / den
            o_ref[:, hs] = o.astype(o_ref.dtype)
            if emit_lse:
                lse_all = jnp.where(lane == hd, m + jnp.log2(den), lse_all)
    if emit_lse:
        lse_ref[...] = lse_all


def band_attn(qkv, *, q_col, k_col, v_col, n_kv, group, half, tl, sink=None, emit_lse=False):
    nb, seq_len, _ = qkv.shape
    wq = n_kv * group * HEAD_DIM
    wk = n_kv * HEAD_DIM
    nq = seq_len // tl
    hb = tl // half
    nh = seq_len // half
    kern = functools.partial(_band_kernel, half=half, tl=tl, seq_len=seq_len, n_kv=n_kv, group=group,
                             has_sink=sink is not None, emit_lse=emit_lse)

    def halo(col):
        prev = pl.BlockSpec((None, half, wk), lambda b, i: (b, jnp.maximum(i * hb - 1, 0), col))
        cur = pl.BlockSpec((None, tl, wk), lambda b, i: (b, i, col))
        nxt = pl.BlockSpec((None, half, wk), lambda b, i: (b, jnp.minimum((i + 1) * hb, nh - 1), col))
        return [prev, cur, nxt]

    in_specs = [pl.BlockSpec((None, tl, wq), lambda b, i: (b, i, q_col))] + halo(k_col) + halo(v_col)
    args = [qkv] * 7
    if sink is not None:
        in_specs = [pl.BlockSpec(memory_space=pltpu.SMEM)] + in_specs
        args = [sink] + args
    out_spec = pl.BlockSpec((None, tl, wq), lambda b, i: (b, i, 0))
    o_shape = jax.ShapeDtypeStruct((nb, seq_len, wq), BF16)
    if emit_lse:
        out_shape = (o_shape, jax.ShapeDtypeStruct((nb, seq_len, LANES), F32))
        out_specs = (out_spec, pl.BlockSpec((None, tl, LANES), lambda b, i: (b, i, 0)))
    else:
        out_shape, out_specs = o_shape, out_spec
    return pl.pallas_call(
        kern,
        out_shape=out_shape,
        grid=(nb, nq),
        in_specs=in_specs,
        out_specs=out_specs,
        compiler_params=_params(("parallel", "arbitrary")),
        name="band_attn",
    )(*args)


def _outproj_kernel(a_ref, w_ref, gate_ref, *refs, n_a):
    x_refs, o_ref = refs[:-1], refs[-1]
    o_ref[...] = _read_rows(x_refs, n_a) + gate_ref[...] * _dot(a_ref[...], w_ref[...])


def outproj(a, w, x, mod3, gate_chunk, seq, *, tm, tn, weight_major=False):
    t, k = a.shape
    d = w.shape[1]
    tps = seq // tm
    nj = d // tn
    xs = _row_sources(x)
    if weight_major:
        assert len(xs) == 1
        grid = (nj, t // tm)
        order = lambda f: (lambda j, i: f(i, j))
    else:
        grid = (t // tm, nj)
        order = lambda f: f
    x_specs = _row_specs(xs, tm, tn, lambda j: j) if not weight_major else [
        pl.BlockSpec((tm, tn), lambda j, i: (i, j))]
    return pl.pallas_call(
        functools.partial(_outproj_kernel, n_a=xs[0].shape[0] // tm),
        out_shape=jax.ShapeDtypeStruct((t, d), F32),
        grid=grid,
        in_specs=[
            pl.BlockSpec((tm, k), order(lambda i, j: (i, 0))),
            pl.BlockSpec((k, tn), order(lambda i, j: (0, j))),
            pl.BlockSpec((None, 1, tn), order(lambda i, j: (i // tps, 0, gate_chunk * nj + j))),
        ] + x_specs,
        out_specs=pl.BlockSpec((tm, tn), order(lambda i, j: (i, j))),
        compiler_params=_params(("parallel", "arbitrary")),
        name="outproj",
    )(a, w, mod3, *xs)


def _merge_outproj_kernel(*refs, dils):
    n_g = len(dils)
    o_refs, l_refs = refs[:n_g], refs[n_g:2 * n_g]
    w_ref, x_ref, gate_ref, o_ref, a_scr = refs[2 * n_g:2 * n_g + 5]
    un_scr = refs[2 * n_g + 5:]
    j = pl.program_id(1)

    @pl.when(j == 0)
    def _():
        n_heads = a_scr.shape[1] // LANES
        slabs, ls = [], []
        si = 0
        for og_ref, lg_ref, dil in zip(o_refs, l_refs, dils):
            if dil == 1:
                slabs.append([og_ref[0, :, c * LANES:(c + 1) * LANES].astype(F32) for c in range(n_heads)])
                ls.append(lg_ref[0])
            else:
                so, sl = un_scr[si], un_scr[si + 1]
                si += 2
                n = og_ref.shape[1]
                for r in range(dil):
                    sl[pl.ds(r, n, stride=dil), :] = lg_ref[r]
                    for c in range(n_heads):
                        so[c, pl.ds(r, n, stride=dil), :] = og_ref[r, :, c * LANES:(c + 1) * LANES].astype(F32)
                slabs.append([so[c] for c in range(n_heads)])
                ls.append(sl[...])
        m = functools.reduce(jnp.maximum, ls)
        es = [jnp.exp2(l - m) for l in ls]
        den = functools.reduce(lambda a, b: a + b, es)
        alphas = [e / den for e in es]
        for c in range(n_heads):
            merged = functools.reduce(lambda a, b: a + b,
                                      [al[:, c:c + 1] * sg[c] for al, sg in zip(alphas, slabs)])
            a_scr[:, c * LANES:(c + 1) * LANES] = merged.astype(BF16)

    o_ref[...] = x_ref[...] + gate_ref[...] * _dot(a_scr[...], w_ref[...])


def merge_outproj(os_, lses, dils, w, x, mod3, gate_chunk, seq, *, tm, tn):
    k = os_[0].shape[-1]
    t, d = x.shape
    tps = seq // tm
    nj = d // tn
    grp = lambda dil, width: pl.BlockSpec((None, dil, tm // dil, width),
                                          lambda i, j: (i // tps, 0, i % tps, 0))
    un_scratch = []
    for dil in dils:
        if dil != 1:
            un_scratch += [pltpu.VMEM((k // LANES, tm, LANES), F32), pltpu.VMEM((tm, LANES), F32)]
    kern = functools.partial(_merge_outproj_kernel, dils=tuple(dils))
    return pl.pallas_call(
        kern,
        out_shape=jax.ShapeDtypeStruct((t, d), F32),
        grid=(t // tm, nj),
        in_specs=[grp(dil, k) for dil in dils] + [grp(dil, LANES) for dil in dils] + [
            pl.BlockSpec((k, tn), lambda i, j: (0, j)),
            pl.BlockSpec((tm, tn), lambda i, j: (i, j)),
            pl.BlockSpec((None, 1, tn), lambda i, j: (i // tps, 0, gate_chunk * nj + j)),
        ],
        out_specs=pl.BlockSpec((tm, tn), lambda i, j: (i, j)),
        scratch_shapes=[pltpu.VMEM((tm, k), BF16)] + un_scratch,
        compiler_params=_params(("parallel", "arbitrary")),
        name="merge_outproj",
    )(*os_, *lses, w, x, mod3)


def _silu_mul(g, u):
    return (g * jax.nn.sigmoid(g)) * u


def _ffn_up_kernel(x_ref, g_ref, sc_ref, sh_ref, wg_ref, wu_ref, o_ref, h_scr):
    j = pl.program_id(1)

    def project(h_rows):
        return _silu_mul(_dot(h_rows, wg_ref[...]), _dot(h_rows, wu_ref[...])).astype(BF16)

    @pl.when(j == 0)
    def _():
        g, sc, sh = g_ref[...], sc_ref[...], sh_ref[...]
        rc = x_ref.shape[0] // NORM_CHUNKS
        for c in range(NORM_CHUNKS):
            rs = slice(c * rc, (c + 1) * rc)
            hb = _norm_mod(x_ref[rs, :], g, sc, sh).astype(BF16)
            h_scr[rs, :] = hb
            o_ref[rs, :] = project(hb)

    @pl.when(j > 0)
    def _():
        o_ref[...] = project(h_scr[...])


def ffn_up(x, mod3, g, w_gu, seq, *, tm, tn):
    t, d = x.shape
    f = w_gu.shape[1] // 2
    tps = seq // tm
    nj = f // tn
    mod_spec = lambda k: pl.BlockSpec((None, 1, d), lambda i, j: (i // tps, 0, k))
    return pl.pallas_call(
        _ffn_up_kernel,
        out_shape=jax.ShapeDtypeStruct((t, f), BF16),
        grid=(t // tm, nj),
        in_specs=[
            pl.BlockSpec((tm, d), lambda i, j: (i, 0)),
            pl.BlockSpec((1, d), lambda i, j: (0, 0)),
            mod_spec(4), mod_spec(3),
            pl.BlockSpec((d, tn), lambda i, j: (0, j)),
            pl.BlockSpec((d, tn), lambda i, j: (0, nj + j)),
        ],
        out_specs=pl.BlockSpec((tm, tn), lambda i, j: (i, j)),
        scratch_shapes=[pltpu.VMEM((tm, d), BF16)],
        compiler_params=_params(("parallel", "arbitrary")),
        name="ffn_up",
    )(x, g.reshape(1, d), mod3, mod3, w_gu, w_gu)


MOE_TM = 512
R_E1, R_E2, R_W1, R_W2, R_RANK1, R_RANK2 = range(6)


def _moe_route_kernel(x_ref, g_ref, sc_ref, sh_ref, r_ref, tri_ref, route_ref, cnt_ref, cnt_scr):
    i = pl.program_id(0)

    @pl.when(i == 0)
    def _():
        cnt_scr[...] = jnp.zeros_like(cnt_scr)

    h = _norm_mod(x_ref[...], g_ref[...], sc_ref[...], sh_ref[...])
    logits = _dot3(h, r_ref[...])
    lane = lax.broadcasted_iota(jnp.int32, logits.shape, 1).astype(F32)
    logits = jnp.where(lane < N_EXPERTS, logits, NEG)
    v1 = jnp.max(logits, axis=1, keepdims=True)
    i1 = jnp.min(jnp.where(logits == v1, lane, float(LANES)), axis=1, keepdims=True)
    rest = jnp.where(lane == i1, NEG, logits)
    v2 = jnp.max(rest, axis=1, keepdims=True)
    i2 = jnp.min(jnp.where(rest == v2, lane, float(LANES)), axis=1, keepdims=True)
    e2 = jnp.exp(v2 - v1)
    w1 = 1.0 / (1.0 + e2)
    w2 = e2 / (1.0 + e2)
    sel = jnp.where(lane == i1, 1.0, 0.0) + jnp.where(lane == i2, 1.0, 0.0)
    base = cnt_scr[0:1, :]
    rank = _dot(tri_ref[...], sel.astype(BF16)) + base
    r1 = jnp.sum(jnp.where(lane == i1, rank, 0.0), axis=1, keepdims=True)
    r2 = jnp.sum(jnp.where(lane == i2, rank, 0.0), axis=1, keepdims=True)
    slab = jnp.zeros_like(logits)
    for col, val in ((R_E1, i1), (R_E2, i2), (R_W1, w1), (R_W2, w2), (R_RANK1, r1), (R_RANK2, r2)):
        slab = jnp.where(lane == col, val, slab)
    route_ref[...] = slab
    total = base + jnp.sum(sel, axis=0, keepdims=True)
    cnt_scr[0:1, :] = total
    cnt_ref[...] = jnp.broadcast_to(total, cnt_ref.shape)


def moe_route(x, mod3, g, router, seq, *, tm):
    t, d = x.shape
    tps = seq // tm
    mod_spec = lambda k: pl.BlockSpec((None, 1, d), lambda i: (i // tps, 0, k))
    router_p = jnp.pad(router, ((0, 0), (0, LANES - router.shape[1])))
    tri = (lax.broadcasted_iota(jnp.int32, (tm, tm), 1)
           < lax.broadcasted_iota(jnp.int32, (tm, tm), 0)).astype(BF16)
    return pl.pallas_call(
        _moe_route_kernel,
        out_shape=(jax.ShapeDtypeStruct((t, LANES), F32), jax.ShapeDtypeStruct((8, LANES), F32)),
        grid=(t // tm,),
        in_specs=[
            pl.BlockSpec((tm, d), lambda i: (i, 0)),
            pl.BlockSpec((1, d), lambda i: (0, 0)),
            mod_spec(4), mod_spec(3),
            pl.BlockSpec((d, LANES), lambda i: (0, 0)),
            pl.BlockSpec((tm, tm), lambda i: (0, 0)),
        ],
        out_specs=(pl.BlockSpec((tm, LANES), lambda i: (i, 0)), pl.BlockSpec((8, LANES), lambda i: (0, 0))),
        scratch_shapes=[pltpu.VMEM((8, LANES), F32)],
        compiler_params=_params(("arbitrary",)),
        name="moe_route",
    )(x, g.reshape(1, d), mod3, mod3, router_p, tri)


def _moe_plan(route, counts, n_tok, tm):
    n_tiles_max = 2 * n_tok // MOE_TM + N_EXPERTS
    cnt = counts[0, :N_EXPERTS].astype(jnp.int32)
    tiles = (cnt + MOE_TM - 1) // MOE_TM
    tile_end = jnp.cumsum(tiles)
    tile_start = tile_end - tiles
    row_start = tile_start * MOE_TM
    n_tiles = tile_end[-1]
    e1 = route[:, R_E1].astype(jnp.int32)
    e2 = route[:, R_E2].astype(jnp.int32)
    pos1 = row_start[e1] + route[:, R_RANK1].astype(jnp.int32)
    pos2 = row_start[e2] + route[:, R_RANK2].astype(jnp.int32)
    shape = (n_tok // tm, 1, tm)
    tile_id = jnp.arange(n_tiles_max, dtype=jnp.int32)
    tile_expert = jnp.minimum(jnp.sum((tile_id[:, None] >= tile_end[None, :]).astype(jnp.int32), axis=1),
                              N_EXPERTS - 1)
    last_tile = jnp.where(tiles > 0, tile_end - 1, -1)
    tail_tile = n_tiles + jnp.arange(N_EXPERTS, dtype=jnp.int32)
    tail_tile = jnp.where(tail_tile < n_tiles_max, tail_tile, -1)
    zero_tiles = jnp.concatenate([last_tile, tail_tile])
    zero_rows = jnp.where(zero_tiles >= 0, zero_tiles * MOE_TM, -1).astype(jnp.int32)
    return pos1.reshape(shape), pos2.reshape(shape), tile_expert.astype(jnp.int32), zero_rows, n_tiles_max


def _moe_dispatch_kernel(zero_ref, pos1_ref, pos2_ref, x_ref, g_ref, sc_ref, sh_ref, xs_ref,
                         h_scr, zero_scr, sem, zsem):
    i = pl.program_id(0)
    tm = x_ref.shape[0]

    @pl.when(i == 0)
    def _():
        zero_scr[...] = jnp.zeros_like(zero_scr)
        for z in range(2 * N_EXPERTS):
            @pl.when(zero_ref[z] >= 0)
            def _():
                row = pl.multiple_of(zero_ref[z], MOE_TM)
                pltpu.make_async_copy(zero_scr, xs_ref.at[pl.ds(row, MOE_TM), :], zsem).start()
        for z in range(2 * N_EXPERTS):
            @pl.when(zero_ref[z] >= 0)
            def _():
                pltpu.make_async_copy(zero_scr, xs_ref.at[pl.ds(0, MOE_TM), :], zsem).wait()

    h_scr[...] = _norm_mod(x_ref[...], g_ref[...], sc_ref[...], sh_ref[...])

    def issue(r, carry):
        src = h_scr.at[pl.ds(r, 1), :]
        pltpu.make_async_copy(src, xs_ref.at[pl.ds(pos1_ref[0, r], 1), :], sem).start()
        pltpu.make_async_copy(src, xs_ref.at[pl.ds(pos2_ref[0, r], 1), :], sem).start()
        return carry

    lax.fori_loop(0, tm, issue, 0, unroll=8)
    for _ in range(2):
        pltpu.make_async_copy(h_scr, xs_ref.at[pl.ds(0, tm), :], sem).wait()


def moe_dispatch(x, mod3, g, pos1, pos2, zero_rows, n_rows, seq, *, tm):
    t, d = x.shape
    tps = seq // tm
    mod_spec = lambda k: pl.BlockSpec((None, 1, d), lambda i, *_: (i // tps, 0, k))
    pos_spec = pl.BlockSpec((None, 1, tm), lambda i, *_: (i, 0, 0), memory_space=pltpu.SMEM)
    return pl.pallas_call(
        _moe_dispatch_kernel,
        out_shape=jax.ShapeDtypeStruct((n_rows, d), F32),
        grid_spec=pltpu.PrefetchScalarGridSpec(
            num_scalar_prefetch=1,
            grid=(t // tm,),
            in_specs=[
                pos_spec, pos_spec,
                pl.BlockSpec((tm, d), lambda i, *_: (i, 0)),
                pl.BlockSpec((1, d), lambda i, *_: (0, 0)),
                mod_spec(4), mod_spec(3),
            ],
            out_specs=pl.BlockSpec(memory_space=pl.ANY),
            scratch_shapes=[pltpu.VMEM((tm, d), F32), pltpu.VMEM((MOE_TM, d), F32),
                            pltpu.SemaphoreType.DMA, pltpu.SemaphoreType.DMA],
        ),
        compiler_params=_params(("arbitrary",)),
        name="moe_dispatch",
    )(zero_rows, pos1, pos2, x, g.reshape(1, d), mod3, mod3)


def _moe_expert_kernel(expert_ref, xs_ref, wgu_ref, wd_ref, ys_ref):
    del expert_ref
    f = wd_ref.shape[0]
    gu = _dot(xs_ref[...].astype(BF16), wgu_ref[...])
    act = _silu_mul(gu[:, :f], gu[:, f:]).astype(BF16)
    ys_ref[...] = _dot(act, wd_ref[...])


def moe_expert(xs, w_gu, w_down, tile_expert):
    n_rows, d = xs.shape
    _, _, two_f = w_gu.shape
    return pl.pallas_call(
        _moe_expert_kernel,
        out_shape=jax.ShapeDtypeStruct((n_rows, d), F32),
        grid_spec=pltpu.PrefetchScalarGridSpec(
            num_scalar_prefetch=1,
            grid=(n_rows // MOE_TM,),
            in_specs=[
                pl.BlockSpec((MOE_TM, d), lambda s, ex: (s, 0)),
                pl.BlockSpec((None, d, two_f), lambda s, ex: (ex[s], 0, 0)),
                pl.BlockSpec((None, two_f // 2, d), lambda s, ex: (ex[s], 0, 0)),
            ],
            out_specs=pl.BlockSpec((MOE_TM, d), lambda s, ex: (s, 0)),
        ),
        compiler_params=_params(("arbitrary",)),
        name="moe_expert",
    )(tile_expert, xs, w_gu, w_down)


def _moe_combine_kernel(pos1_ref, pos2_ref, npos1_ref, npos2_ref, x_ref, route_ref, gate_ref, ys_ref,
                        *refs, n_a):
    if n_a is None:
        o_ref, y1_scr, y2_scr, sem = refs
    else:
        fg_ref, oa_ref, ob_ref, y1_scr, y2_scr, sem = refs
    tm = x_ref.shape[0]
    i = pl.program_id(0)
    slot = i % 2

    def gather(p1_ref, p2_ref, s):
        def issue(r, carry):
            pltpu.make_async_copy(ys_ref.at[pl.ds(p1_ref[0, r], 1), :], y1_scr.at[s, pl.ds(r, 1), :],
                                  sem.at[s]).start()
            pltpu.make_async_copy(ys_ref.at[pl.ds(p2_ref[0, r], 1), :], y2_scr.at[s, pl.ds(r, 1), :],
                                  sem.at[s]).start()
            return carry

        lax.fori_loop(0, tm, issue, 0, unroll=8)

    @pl.when(i == 0)
    def _():
        gather(pos1_ref, pos2_ref, 0)

    @pl.when(i + 1 < pl.num_programs(0))
    def _():
        gather(npos1_ref, npos2_ref, 1 - slot)

    for buf in (y1_scr, y2_scr):
        pltpu.make_async_copy(ys_ref.at[pl.ds(0, tm), :], buf.at[slot], sem.at[slot]).wait()
    route = route_ref[...]
    w1 = route[:, R_W1:R_W1 + 1]
    w2 = route[:, R_W2:R_W2 + 1]
    x = x_ref[...] + gate_ref[...] * (w1 * y1_scr[slot] + w2 * y2_scr[slot])
    if n_a is None:
        o_ref[...] = x
        return
    ms = jnp.mean(x * x, axis=-1, keepdims=True)
    y = (x * lax.rsqrt(ms + NORM_EPS)) * fg_ref[...]
    i = pl.program_id(0)

    @pl.when(i < n_a)
    def _():
        oa_ref[...] = y

    @pl.when(i >= n_a)
    def _():
        ob_ref[...] = y


def moe_combine(x, mod3, route, pos1, pos2, ys, seq, *, tm, final=None):
    t, d = x.shape
    tps = seq // tm
    last = t // tm - 1
    pos_spec = pl.BlockSpec((None, 1, tm), lambda i: (i, 0, 0), memory_space=pltpu.SMEM)
    next_spec = pl.BlockSpec((None, 1, tm), lambda i: (jnp.minimum(i + 1, last), 0, 0),
                             memory_space=pltpu.SMEM)
    in_specs = [
        pos_spec, pos_spec, next_spec, next_spec,
        pl.BlockSpec((tm, d), lambda i: (i, 0)),
        pl.BlockSpec((tm, LANES), lambda i: (i, 0)),
        pl.BlockSpec((None, 1, d), lambda i: (i // tps, 0, 5)),
        pl.BlockSpec(memory_space=pl.ANY),
    ]
    args = [pos1, pos2, pos1, pos2, x, route, mod3, ys]
    if final is None:
        n_a = None
        out_shape = jax.ShapeDtypeStruct((t, d), F32)
        out_specs = pl.BlockSpec((tm, d), lambda i: (i, 0))
    else:
        final_g, rows_a = final
        n_a = rows_a // tm
        in_specs.append(pl.BlockSpec((1, d), lambda i: (0, 0)))
        args.append(final_g.reshape(1, d))
        out_shape = (jax.ShapeDtypeStruct((rows_a, d), F32), jax.ShapeDtypeStruct((t - rows_a, d), F32))
        out_specs = (pl.BlockSpec((tm, d), lambda i: (jnp.minimum(i, n_a - 1), 0)),
                     pl.BlockSpec((tm, d), lambda i: (jnp.maximum(i - n_a, 0), 0)))
    return pl.pallas_call(
        functools.partial(_moe_combine_kernel, n_a=n_a),
        out_shape=out_shape,
        grid=(t // tm,),
        in_specs=in_specs,
        out_specs=out_specs,
        scratch_shapes=[pltpu.VMEM((2, tm, d), F32), pltpu.VMEM((2, tm, d), F32),
                        pltpu.SemaphoreType.DMA((2,))],
        compiler_params=_params(("arbitrary",)),
        name="moe_combine",
    )(*args)


def moe_ffn(x, mod3, g, router, w_gu, w_down, seq, *, tm, final=None):
    t = x.shape[0]
    route, counts = moe_route(x, mod3, g, router, seq, tm=tm)
    pos1, pos2, tile_expert, zero_rows, n_tiles_max = _moe_plan(route, counts, t, tm)
    xs = moe_dispatch(x, mod3, g, pos1, pos2, zero_rows, n_tiles_max * MOE_TM, seq, tm=tm)
    ys = moe_expert(xs, w_gu, w_down, tile_expert)
    return moe_combine(x, mod3, route, pos1, pos2, ys, seq, tm=tm, final=final)


def _lambda_init(layer):
    return 0.8 - 0.6 * math.exp(-0.3 * layer)


def _mixer_a(x, mod3, g, w_in, w_out, lam_p, subln, tables, nb, seq, layer, cfg):
    tm_in = cfg["tm"] if len(_row_sources(x)) > 1 else cfg["tm_in"]
    qkv = inproj(x, mod3, g, w_in, tables, seq, q_cols=D_MODEL, rope_cols=2 * D_MODEL,
                 tm=tm_in, tn=cfg["tn_in"]).reshape(nb * seq, 3 * D_MODEL)
    o = attn_a(qkv, lam_p, subln, nb, seq, _lambda_init(layer), tq=cfg["tq_a"], tk=cfg["tk_a"],
               tk_pv=cfg["tk_pv_a"], unroll=cfg["unroll_a"])
    return outproj(o, w_out, x, mod3, 2, seq, tm=cfg["tm"], tn=cfg["tn_out_mix"])


def _mixer_b(x, mod3, g, w_in, w_out, sink, tables, nb, seq, cfg):
    nq = B_HEADS * HEAD_DIM
    qkv = inproj(x, mod3, g, w_in, tables, seq, q_cols=nq, rope_cols=nq + B_KV_HEADS * HEAD_DIM,
                 tm=cfg["tm_in"], tn=cfg["tn_in_b"]).reshape(nb, seq, -1)
    o = band_attn(qkv, q_col=0, k_col=4, v_col=5, n_kv=B_KV_HEADS, group=B_HEADS // B_KV_HEADS,
                  half=B_WINDOW, tl=cfg["tl_b"], sink=sink)
    return outproj(o.reshape(nb * seq, nq), w_out, x, mod3, 2, seq, tm=cfg["tm"], tn=cfg["tn_out_mix"])


def _mixer_c(x, mod3, g, w_in, w_out, tables, nb, seq, cfg):
    n_groups = len(C_CONFIGS)
    os_, lses, dils = [], [], []
    for gi, (window, dil) in enumerate(C_CONFIGS):
        qkv = inproj(x, mod3, g, w_in, tables, seq, q_cols=C_WIDTH, rope_cols=2 * C_WIDTH,
                     tm=cfg["tm_in"], tn=C_WIDTH, dil=dil, col_stride=n_groups, col_off=gi,
                     n_cols=3 * C_WIDTH)
        sub = seq // dil
        o, lse = band_attn(qkv.reshape(nb * dil, sub, 3 * C_WIDTH), q_col=0, k_col=1, v_col=2,
                           n_kv=C_HEADS, group=1, half=window // (2 * dil), tl=cfg["tl_c"],
                           emit_lse=True)
        os_.append(o.reshape(nb, dil, sub, C_WIDTH))
        lses.append(lse.reshape(nb, dil, sub, LANES))
        dils.append(dil)
    return merge_outproj(os_, lses, dils, w_out, x, mod3, 2, seq, tm=cfg["tm_merge"],
                         tn=cfg["tn_out_mix"])


def _default_cfg(seq):
    return dict(tm=min(512, seq), tm_in=min(1024, seq), tm_up=min(1024, seq), tn_in=1024, tn_in_b=512,
                tn_out_mix=D_MODEL, tn_out_ffn=1024,
                tq_a=min(512, seq), tk_a=min(512, seq), tl_b=min(256, seq),
                tl_c=min(128, seq // C_CONFIGS[-1][1]), tm_merge=min(512, seq), tn_up=512,
                unroll_a=True, tk_pv_a=min(2048, seq))


def trunk_all(x, c8, nb, seq, ada_w, ada_b, norm_mix, norm_ffn, a_w_in, a_w_out, a_lambda, a_subln,
              b_w_in, b_w_out, b_sink, c_w_in, c_w_out, f_w_gu, f_w_down, moe_router, moe_w_gu,
              moe_w_down, final_norm_g, cfg=None):
    assert (DEPTH - 1) % 2 == 1, "the final norm is fused into the last layer's routed-FFN combine"
    rows_a = x[0].shape[0]
    cfg = cfg or _default_cfg(seq)
    bf = lambda w: w.astype(BF16)
    mod_all = adaln(c8, ada_w, ada_b)
    tables = _rope_tables(seq)
    for i in range(DEPTH):
        mod3 = mod_all[i].reshape(8, 1, 6 * D_MODEL)
        kind, jm = i % 3, i // 3
        if kind == 0:
            x = _mixer_a(x, mod3, norm_mix[i], bf(a_w_in[jm]), bf(a_w_out[jm]), a_lambda[jm],
                         a_subln[jm], tables, nb, seq, i, cfg)
        elif kind == 1:
            x = _mixer_b(x, mod3, norm_mix[i], bf(b_w_in[jm]), bf(b_w_out[jm]), b_sink[jm], tables,
                         nb, seq, cfg)
        else:
            x = _mixer_c(x, mod3, norm_mix[i], bf(c_w_in[jm]), bf(c_w_out[jm]), tables, nb, seq, cfg)
        jf = i // 2
        if i % 2 == 0:
            act = ffn_up(x, mod3, norm_ffn[i], bf(f_w_gu[jf]), seq, tm=cfg["tm_up"], tn=cfg["tn_up"])
            x = outproj(act, bf(f_w_down[jf]), x, mod3, 5, seq, tm=cfg["tm"], tn=cfg["tn_out_ffn"],
                        weight_major=True)
        else:
            x = moe_ffn(x, mod3, norm_ffn[i], moe_router[jf], bf(moe_w_gu[jf]), bf(moe_w_down[jf]), seq,
                        tm=cfg["tm"], final=(final_norm_g, rows_a) if i == DEPTH - 1 else None)
    return x


def kernel(x_prompt, x_sample, c_prompt, c_sample, ada_w, ada_b, norm_mix, norm_ffn, a_w_in, a_w_out,
           a_lambda, a_subln, b_w_in, b_w_out, b_sink, c_w_in, c_w_out, f_w_gu, f_w_down, moe_router,
           moe_w_gu, moe_w_down, final_norm):
    bp, seq, d = x_prompt.shape
    bs = x_sample.shape[0]
    assert x_sample.shape[1] == seq
    nb = bp + bs
    x = (x_prompt.reshape(bp * seq, d), x_sample.reshape(bs * seq, d))
    c8 = jnp.concatenate([c_prompt, c_sample, jnp.zeros((8 - nb, d), F32)], axis=0)
    ya, yb = trunk_all(x, c8, nb, seq, ada_w, ada_b, norm_mix, norm_ffn, a_w_in, a_w_out, a_lambda,
                       a_subln, b_w_in, b_w_out, b_sink, c_w_in, c_w_out, f_w_gu, f_w_down, moe_router,
                       moe_w_gu, moe_w_down, final_norm)
    return (ya.reshape(bp, seq, d), yb.reshape(bs, seq, d))
```
